```python
import math
import jax, jax.numpy as jnp
from jax import lax
import numpy as np

D_MODEL = 1024
BATCH = 2
SEQ = 8192
DEPTH = 2
DEC_BATCH = 128
DEC_SEQ = 1
PAST_LEN = 2048
PAGE_SIZE = 128

HEAD_DIM = 64
H_A = 8
H_B = 8
W_A = H_A * HEAD_DIM
W_B = H_B * HEAD_DIM
CONV_W = 4
DELTA_CHUNK = 64
Q_BLOCK = 128
H_C = D_MODEL // HEAD_DIM
LORA_W = 64
LORA_A = 64
LORA_G = 128
N_GROUPS = 4
EXP_PER_GROUP = 8
N_EXPERTS = N_GROUPS * EXP_PER_GROUP
D_EXPERT = D_MODEL // 4
TOP_K = 2
N_AB_LAYERS = (DEPTH + 1) // 2
N_RW_LAYERS = DEPTH // 2
IN_SIZES = (W_A, W_A, W_A, H_A, W_B, W_B, W_B, W_B, H_B, H_B)
IN_COLS = sum(IN_SIZES)
IN_OFFSETS = tuple(int(v) for v in np.cumsum(IN_SIZES)[:-1])
RMS_EPS = 1e-6
GN_EPS = 64e-5
F32 = jnp.float32

kernel_name = 'hybrid_fox_gdn_rwkv7_hmoe_step'


def rmsnorm(x, g):
    xf = x.astype(F32)
    y = xf * lax.rsqrt(jnp.mean(xf * xf, axis=-1, keepdims=True) + RMS_EPS)
    return (y * g.astype(F32)).astype(x.dtype)


def l2norm(x):
    xf = x.astype(F32)
    return xf * lax.rsqrt(jnp.sum(xf * xf, axis=-1, keepdims=True) + RMS_EPS)


def split_heads(t):
    return t.reshape(t.shape[:-1] + (-1, HEAD_DIM))


def gather_pages(pool, page_table):
    g = pool[page_table]
    return g.reshape((g.shape[0], g.shape[1] * g.shape[2]) + g.shape[3:])


def fox_attend(q, k, v, dq, dk, q_pos, k_pos):
    s = jnp.einsum('bqhd,bkhd->bhqk', q, k).astype(F32) * (HEAD_DIM ** -0.5)
    bias = dq.astype(F32).transpose(0, 2, 1)[:, :, :, None] - dk.astype(F32).transpose(0, 2, 1)[:, :, None, :]
    mask = k_pos[None, :] <= q_pos[:, None]
    p = jax.nn.softmax(jnp.where(mask, s + bias, -jnp.inf), axis=-1)
    return jnp.einsum('bhqk,bkhd->bqhd', p.astype(v.dtype), v)


def fox_prompt(q, k, v, lf):
    b, s = q.shape[:2]
    d = jnp.cumsum(lf.astype(F32), axis=1)
    nb = s // Q_BLOCK
    qb = q.reshape(b, nb, Q_BLOCK, H_A, HEAD_DIM).swapaxes(0, 1)
    db = d.reshape(b, nb, Q_BLOCK, H_A).swapaxes(0, 1)
    k_pos = jnp.arange(s)

    def one_block(args):
        qi, di, i = args
        q_pos = i * Q_BLOCK + jnp.arange(Q_BLOCK)
        return fox_attend(qi, k, v, di, d, q_pos, k_pos)

    o = lax.map(one_block, (qb, db, jnp.arange(nb)))
    return o.swapaxes(0, 1).reshape(b, s, H_A, HEAD_DIM)


def fox_sample(q, k, v, lf, k_past, v_past, lf_past):
    past_len = k_past.shape[1]
    t = q.shape[1]
    k_all = jnp.concatenate([k_past.astype(k.dtype), k], axis=1)
    v_all = jnp.concatenate([v_past.astype(v.dtype), v], axis=1)
    d_all = jnp.cumsum(jnp.concatenate([lf_past.astype(F32), lf.astype(F32)], axis=1), axis=1)
    q_pos = past_len + jnp.arange(t)
    k_pos = jnp.arange(past_len + t)
    return fox_attend(q, k_all, v_all, d_all[:, past_len:], d_all, q_pos, k_pos)


def causal_dwconv(x_ext, w):
    t = x_ext.shape[1] - (CONV_W - 1)
    return sum(x_ext[:, i:i + t] * w[i] for i in range(CONV_W))


def gated_delta_chunked(q, k, v, log_g, beta, s0):
    b, t, h, _ = q.shape
    c = min(DELTA_CHUNK, t)
    pad = (-t) % c
    n = (t + pad) // c

    def blocks(a):
        a = a.astype(F32)
        if pad:
            a = jnp.pad(a, [(0, 0), (0, pad)] + [(0, 0)] * (a.ndim - 2))
        a = a.reshape((b, n, c) + a.shape[2:])
        return jnp.moveaxis(a, 3, 2).swapaxes(0, 1)

    qc, kc, vc, gc, bc = blocks(q), blocks(k), blocks(v), blocks(log_g), blocks(beta)
    lg = jnp.cumsum(gc, axis=-1)
    diff = lg[..., :, None] - lg[..., None, :]
    idx = jnp.arange(c)
    incl = idx[:, None] >= idx[None, :]
    strict = idx[:, None] > idx[None, :]
    decay = jnp.exp(jnp.where(incl, diff, -jnp.inf))
    kk = jnp.einsum('...id,...jd->...ij', kc, kc)
    lower = jnp.where(strict, bc[..., :, None] * kk * decay, 0.0) + jnp.eye(c, dtype=F32)
    w_blk = lax.linalg.triangular_solve(lower, (bc * jnp.exp(lg))[..., None] * kc,
                                        left_side=True, lower=True, unit_diagonal=True)
    u_blk = lax.linalg.triangular_solve(lower, bc[..., None] * vc,
                                        left_side=True, lower=True, unit_diagonal=True)
    qk = jnp.einsum('...td,...id->...ti', qc, kc) * decay
    q_dec = qc * jnp.exp(lg)[..., None]
    k_dec = kc * jnp.exp(lg[..., -1:] - lg)[..., None]
    g_tot = jnp.exp(lg[..., -1])

    def step(s, xs):
        w_c, u_c, qk_c, qd_c, kd_c, g_c = xs
        u = u_c - jnp.einsum('bhck,bhkv->bhcv', w_c, s)
        o = jnp.einsum('bhck,bhkv->bhcv', qd_c, s) + jnp.einsum('bhti,bhiv->bhtv', qk_c, u)
        s = g_c[..., None, None] * s + jnp.einsum('bhck,bhcv->bhkv', kd_c, u)
        return s, o

    s_fin, o = lax.scan(step, s0.astype(F32), (w_blk, u_blk, qk, q_dec, k_dec, g_tot))
    o = jnp.moveaxis(o.swapaxes(0, 1), 2, 3).reshape(b, n * c, h, -1)[:, :t]
    return o, s_fin


def ab_mixer(h, p, past):
    b, t, _ = h.shape
    z = h @ p['w_in']
    qa, ka, va, fa, qb, kb, vb, gb, beta_in, alpha_in = jnp.split(z, IN_OFFSETS, axis=-1)
    qa, ka, va = split_heads(qa), split_heads(ka), split_heads(va)
    lf = jax.nn.log_sigmoid((fa + p['b_f']).astype(F32))
    qkv_b = jnp.concatenate([qb, kb, vb], axis=-1)
    if past is None:
        o_a = fox_prompt(qa, ka, va, lf)
        conv_in = jnp.pad(qkv_b, ((0, 0), (CONV_W - 1, 0), (0, 0)))
        s0 = jnp.zeros((b, H_B, HEAD_DIM, HEAD_DIM), F32)
    else:
        k_past, v_past, lf_past, s0, conv_buf = past
        o_a = fox_sample(qa, ka, va, lf, k_past, v_past, lf_past)
        conv_in = jnp.concatenate([conv_buf.astype(qkv_b.dtype), qkv_b], axis=1)
    conv_out = jax.nn.silu(causal_dwconv(conv_in, p['conv_w']))
    new_conv = conv_in[:, -(CONV_W - 1):]
    qb, kb, vb = jnp.split(conv_out, 3, axis=-1)
    qb = l2norm(split_heads(qb)) * (HEAD_DIM ** -0.5)
    kb = l2norm(split_heads(kb))
    vb = split_heads(vb)
    beta = jax.nn.sigmoid(beta_in.astype(F32))
    log_g = -jnp.exp(p['a_log'].astype(F32)) * jax.nn.softplus((alpha_in + p['dt_bias']).astype(F32))
    o_b, s_new = gated_delta_chunked(qb, kb, vb, log_g, beta, s0)
    o_b = rmsnorm(o_b, p['o_norm']) * jax.nn.silu(split_heads(gb).astype(F32))
    o = jnp.concatenate([o_a.reshape(b, t, W_A).astype(h.dtype), o_b.reshape(b, t, W_B).astype(h.dtype)], axis=-1)
    return o @ p['w_out'], (ka, va, lf, s_new, new_conv)


def rwkv7_mixer(h, prev, s0, p):
    b, t, d = h.shape
    h_prev = jnp.concatenate([prev[:, None, :].astype(h.dtype), h[:, :-1]], axis=1)
    xx = h_prev - h
    mu = p['mu']
    xr, xw, xk, xv, xa, xg = (h + xx * mu[i] for i in range(6))
    r = xr @ p['w_rkv'][0]
    k = xk @ p['w_rkv'][1]
    v = xv @ p['w_rkv'][2]
    w_log = -jax.nn.softplus(-(p['w0'] + jnp.tanh(xw @ p['w1']) @ p['w2']).astype(F32)) - 0.5
    decay = jnp.exp(-jnp.exp(w_log))
    a = jax.nn.sigmoid((p['a0'] + (xa @ p['a1']) @ p['a2']).astype(F32))
    g = jax.nn.sigmoid(xg @ p['g1']) @ p['g2']
    kk = l2norm(split_heads(k * p['k_k']))
    k = k * (1 + (a - 1) * p['k_a'])
    r_h = split_heads(r).astype(F32)
    k_h = split_heads(k).astype(F32)
    v_h = split_heads(v).astype(F32)
    w_h, a_h = split_heads(decay), split_heads(a)

    def tm(z):
        return jnp.moveaxis(z, 1, 0)

    def step(s, xs):
        r_t, w_t, k_t, v_t, kk_t, a_t = xs
        sa = jnp.einsum('bhvk,bhk->bhv', s, -kk_t)
        s = s * w_t[:, :, None, :] + sa[..., None] * (kk_t * a_t)[:, :, None, :] + v_t[..., None] * k_t[:, :, None, :]
        return s, jnp.einsum('bhvk,bhk->bhv', s, r_t)

    s_fin, y = lax.scan(step, s0.astype(F32), (tm(r_h), tm(w_h), tm(k_h), tm(v_h), tm(kk), tm(a_h)))
    y = tm(y)
    mean = jnp.mean(y, axis=-1, keepdims=True)
    var = jnp.mean(jnp.square(y - mean), axis=-1, keepdims=True)
    y = ((y - mean) * lax.rsqrt(var + GN_EPS)).reshape(b, t, d) * p['gn_w'] + p['gn_b']
    bonus = jnp.sum(r_h * k_h * p['r_k'], axis=-1, keepdims=True) * v_h
    y = y + bonus.reshape(b, t, d)
    out = (y * g).astype(h.dtype) @ p['w_out']
    return out, s_fin, h[:, -1]


def hier_moe(h, p):
    b, t, d = h.shape
    x = h.reshape(b * t, d)
    n = x.shape[0]
    rows = jnp.arange(n)
    g_logit = (x @ p['w_grp']).astype(F32) + p['b_grp']
    grp = jnp.argmax(g_logit, axis=-1)
    g_prob = jax.nn.softmax(g_logit, axis=-1)[rows, grp][:, None]
    e_logit = ((x @ p['w_exp']).astype(F32) + p['b_exp']).reshape(n, N_GROUPS, EXP_PER_GROUP)
    e_sel = e_logit[rows, grp]
    top_v, top_i = lax.top_k(e_sel, TOP_K)
    top_w = jax.nn.softmax(top_v, axis=-1) * g_prob
    expert_id = grp[:, None] * EXP_PER_GROUP + top_i
    comb = jnp.sum(jax.nn.one_hot(expert_id, N_EXPERTS, dtype=F32) * top_w[..., None], axis=1).astype(x.dtype)
    y = jnp.zeros_like(x)
    for gi in range(N_GROUPS):
        lo, hi = gi * EXP_PER_GROUP, (gi + 1) * EXP_PER_GROUP
        hid = jax.nn.silu(jnp.einsum('nd,edf->nef', x, p['w_gate'][lo:hi])) * jnp.einsum('nd,edf->nef', x, p['w_up'][lo:hi])
        y = y + jnp.einsum('nef,efd->nd', hid * comb[:, lo:hi, None], p['w_down'][lo:hi])
    return y.reshape(b, t, d)


def trunk(x, c, past, P):
    ab_states, rw_states = [], []
    for layer in range(DEPTH):
        mod = jax.nn.silu(c) @ P['w_mod'][layer] + P['b_mod'][layer]
        sh1, sc1, g1, sh2, sc2, g2 = jnp.split(mod[:, None, :], 6, axis=-1)
        h = rmsnorm(x, P['norm_mix'][layer]) * (1 + sc1) + sh1
        if layer % 2 == 0:
            i = layer // 2
            lp = {name: arr[i] for name, arr in P['ab'].items()}
            if past is None:
                lpast = None
            else:
                pt = past['page_table']
                lpast = (gather_pages(past['cache_k'][i], pt), gather_pages(past['cache_v'][i], pt),
                         gather_pages(past['cache_lf'][i], pt), past['ssm'][i], past['conv'][i])
            out, st = ab_mixer(h, lp, lpast)
            ab_states.append(st)
        else:
            j = layer // 2
            lp = {name: arr[j] for name, arr in P['rw'].items()}
            if past is None:
                prev = jnp.zeros((x.shape[0], D_MODEL), x.dtype)
                s0 = jnp.zeros((x.shape[0], H_C, HEAD_DIM, HEAD_DIM), F32)
            else:
                prev, s0 = past['shift'][j], past['wkv'][j]
            out, s_fin, last = rwkv7_mixer(h, prev, s0, lp)
            rw_states.append((s_fin, last))
        x = x + g1 * out
        h = rmsnorm(x, P['norm_ffn'][layer]) * (1 + sc2) + sh2
        mp = {name: arr[layer] for name, arr in P['moe'].items()}
        x = x + g2 * hier_moe(h, mp)
    return rmsnorm(x, P['norm_final']), ab_states, rw_states


def stack_states(states):
    return tuple(jnp.stack([s[m] for s in states]) for m in range(len(states[0])))


def setup_inputs(seed: int = 0) -> dict:
    key = jax.random.key(seed)
    keys = iter(jax.random.split(key, 96))

    def nrm(shape, scale):
        return jax.random.normal(next(keys), shape, F32) * scale

    def unif(shape, lo, hi):
        return jax.random.uniform(next(keys), shape, F32, lo, hi)

    D = D_MODEL
    NA, NR = N_AB_LAYERS, N_RW_LAYERS
    n_pages = PAST_LEN // PAGE_SIZE
    n_used = DEC_BATCH * n_pages
    n_pool = n_used + n_used // 4
    page_table = jax.random.permutation(next(keys), n_pool)[:n_used].reshape(DEC_BATCH, n_pages).astype(jnp.int32)
    dt = jnp.exp(unif((NA, H_B), math.log(1e-3), math.log(1e-1)))
    return {
        'x_prompt': nrm((BATCH, SEQ, D), 1.0),
        'x_sample': nrm((DEC_BATCH, DEC_SEQ, D), 1.0),
        'c_prompt': nrm((BATCH, D), 1.0),
        'c_sample': nrm((DEC_BATCH, D), 1.0),
        'page_table': page_table,
        'cache_k_a': nrm((NA, n_pool, PAGE_SIZE, H_A, HEAD_DIM), 1.0),
        'cache_v_a': nrm((NA, n_pool, PAGE_SIZE, H_A, HEAD_DIM), 1.0),
        'cache_lf_a': jax.nn.log_sigmoid(3.0 + nrm((NA, n_pool, PAGE_SIZE, H_A), 1.0)),
        'state_ssm_b': nrm((NA, DEC_BATCH, H_B, HEAD_DIM, HEAD_DIM), 0.1),
        'state_conv_b': nrm((NA, DEC_BATCH, CONV_W - 1, 3 * W_B), 1.0),
        'state_wkv_c': nrm((NR, DEC_BATCH, H_C, HEAD_DIM, HEAD_DIM), 0.1),
        'state_shift_c': nrm((NR, DEC_BATCH, D), 1.0),
        'w_mod': nrm((DEPTH, D, 6 * D), 0.5 * D ** -0.5),
        'b_mod': nrm((DEPTH, 6 * D), 0.01),
        'norm_mix': 1.0 + nrm((DEPTH, D), 0.02),
        'norm_ffn': 1.0 + nrm((DEPTH, D), 0.02),
        'norm_final': 1.0 + nrm((D,), 0.02),
        'ab_w_in': nrm((NA, D, IN_COLS), D ** -0.5),
        'ab_b_f': 3.0 + nrm((NA, H_A), 0.5),
        'ab_conv_w': nrm((NA, CONV_W, 3 * W_B), CONV_W ** -0.5),
        'ab_a_log': jnp.log(unif((NA, H_B), 1.0, 16.0)),
        'ab_dt_bias': dt + jnp.log(-jnp.expm1(-dt)),
        'ab_o_norm': 1.0 + nrm((NA, HEAD_DIM), 0.02),
        'ab_w_out': nrm((NA, W_A + W_B, D), (W_A + W_B) ** -0.5),
        'rw_mu': unif((NR, 6, D), 0.0, 1.0),
        'rw_w_rkv': nrm((NR, 3, D, D), D ** -0.5),
        'rw_w0': unif((NR, D), -6.0, -1.0),
        'rw_w1': nrm((NR, D, LORA_W), D ** -0.5),
        'rw_w2': nrm((NR, LORA_W, D), 0.1 * LORA_W ** -0.5),
        'rw_a0': nrm((NR, D), 0.1),
        'rw_a1': nrm((NR, D, LORA_A), D ** -0.5),
        'rw_a2': nrm((NR, LORA_A, D), 0.5 * LORA_A ** -0.5),
        'rw_g1': nrm((NR, D, LORA_G), D ** -0.5),
        'rw_g2': nrm((NR, LORA_G, D), LORA_G ** -0.5),
        'rw_k_k': 0.85 + nrm((NR, D), 0.02),
        'rw_k_a': 1.0 + nrm((NR, D), 0.02),
        'rw_r_k': nrm((NR, H_C, HEAD_DIM), 0.1),
        'rw_gn_w': 1.0 + nrm((NR, D), 0.02),
        'rw_gn_b': nrm((NR, D), 0.01),
        'rw_w_out': nrm((NR, D, D), D ** -0.5),
        'moe_w_grp': nrm((DEPTH, D, N_GROUPS), D ** -0.5),
        'moe_b_grp': nrm((DEPTH, N_GROUPS), 0.01),
        'moe_w_exp': nrm((DEPTH, D, N_EXPERTS), D ** -0.5),
        'moe_b_exp': nrm((DEPTH, N_EXPERTS), 0.01),
        'moe_w_gate': nrm((DEPTH, N_EXPERTS, D, D_EXPERT), D ** -0.5),
        'moe_w_up': nrm((DEPTH, N_EXPERTS, D, D_EXPERT), D ** -0.5),
        'moe_w_down': nrm((DEPTH, N_EXPERTS, D_EXPERT, D), D_EXPERT ** -0.5),
    }


def reference(x_prompt, x_sample, c_prompt, c_sample, page_table,
              cache_k_a, cache_v_a, cache_lf_a, state_ssm_b, state_conv_b, state_wkv_c, state_shift_c,
              w_mod, b_mod, norm_mix, norm_ffn, norm_final,
              ab_w_in, ab_b_f, ab_conv_w, ab_a_log, ab_dt_bias, ab_o_norm, ab_w_out,
              rw_mu, rw_w_rkv, rw_w0, rw_w1, rw_w2, rw_a0, rw_a1, rw_a2, rw_g1, rw_g2,
              rw_k_k, rw_k_a, rw_r_k, rw_gn_w, rw_gn_b, rw_w_out,
              moe_w_grp, moe_b_grp, moe_w_exp, moe_b_exp, moe_w_gate, moe_w_up, moe_w_down):
    P = {
        'w_mod': w_mod, 'b_mod': b_mod, 'norm_mix': norm_mix, 'norm_ffn': norm_ffn, 'norm_final': norm_final,
        'ab': {'w_in': ab_w_in, 'b_f': ab_b_f, 'conv_w': ab_conv_w, 'a_log': ab_a_log,
               'dt_bias': ab_dt_bias, 'o_norm': ab_o_norm, 'w_out': ab_w_out},
        'rw': {'mu': rw_mu, 'w_rkv': rw_w_rkv, 'w0': rw_w0, 'w1': rw_w1, 'w2': rw_w2,
               'a0': rw_a0, 'a1': rw_a1, 'a2': rw_a2, 'g1': rw_g1, 'g2': rw_g2,
               'k_k': rw_k_k, 'k_a': rw_k_a, 'r_k': rw_r_k, 'gn_w': rw_gn_w, 'gn_b': rw_gn_b,
               'w_out': rw_w_out},
        'moe': {'w_grp': moe_w_grp, 'b_grp': moe_b_grp, 'w_exp': moe_w_exp, 'b_exp': moe_b_exp,
                'w_gate': moe_w_gate, 'w_up': moe_w_up, 'w_down': moe_w_down},
    }
    y_prompt, ab_p, rw_p = trunk(x_prompt, c_prompt, None, P)
    past = {'page_table': page_table, 'cache_k': cache_k_a, 'cache_v': cache_v_a, 'cache_lf': cache_lf_a,
            'ssm': state_ssm_b, 'conv': state_conv_b, 'wkv': state_wkv_c, 'shift': state_shift_c}
    y_sample, ab_s, rw_s = trunk(x_sample, c_sample, past, P)
    k_a_prompt, v_a_prompt, lf_a_prompt, ssm_b_prompt, conv_b_prompt = stack_states(ab_p)
    wkv_c_prompt, shift_c_prompt = stack_states(rw_p)
    k_a_sample, v_a_sample, lf_a_sample, ssm_b_sample, conv_b_sample = stack_states(ab_s)
    wkv_c_sample, shift_c_sample = stack_states(rw_s)
    return (y_prompt, y_sample,
            k_a_prompt, v_a_prompt, lf_a_prompt, ssm_b_prompt, conv_b_prompt, wkv_c_prompt, shift_c_prompt,
            k_a_sample, v_a_sample, lf_a_sample, ssm_b_sample, conv_b_sample, wkv_c_sample, shift_c_sample)
```

```python
import functools
import math

import jax
import jax.numpy as jnp
import numpy as np
from jax import lax
from jax.experimental import pallas as pl
from jax.experimental.pallas import tpu as pltpu

F32 = jnp.float32
BF16 = jnp.bfloat16

D_MODEL = 1024
HEAD_DIM = 64
H_A = 8
H_B = 8
W_A = H_A * HEAD_DIM
W_B = H_B * HEAD_DIM
CONV_W = 4
H_C = D_MODEL // HEAD_DIM
N_GROUPS = 4
EXP_PER_GROUP = 8
N_EXPERTS = N_GROUPS * EXP_PER_GROUP
D_EXPERT = D_MODEL // 4
PAGE_SIZE = 128
IN_SIZES = (W_A, W_A, W_A, H_A, W_B, W_B, W_B, W_B, H_B, H_B)
IN_OFFSETS = tuple(int(v) for v in np.cumsum(IN_SIZES)[:-1])
RMS_EPS = 1e-6
GN_EPS = 64e-5

LANES = 128
SUBLANES = 8
CHUNK = 64
VMEM_LIMIT = 48 * 1024 * 1024

_NN = (((1,), (0,)), ((), ()))
_NT = (((1,), (1,)), ((), ()))
_TN = (((0,), (0,)), ((), ()))


def _round_up(n, m):
    return (n + m - 1) // m * m


def _params(*sem):
    return pltpu.CompilerParams(dimension_semantics=sem, vmem_limit_bytes=VMEM_LIMIT)


def _dot(a, b, dims=_NN):
    return lax.dot_general(a.astype(BF16), b.astype(BF16), dims, preferred_element_type=F32)


def _split2(a):
    hi = a.astype(BF16)
    lo = (a - hi.astype(F32)).astype(BF16)
    return hi, lo


def _dot3(a, b, dims=_NN):
    ah, al = _split2(a)
    bh, bl = _split2(b)
    d = functools.partial(lax.dot_general, dimension_numbers=dims, preferred_element_type=F32)
    return d(ah, bh) + (d(ah, bl) + d(al, bh))


def _dot_exact_rhs(a, ones_b, dims=_NN):
    hi = a.astype(BF16)
    r = a - hi.astype(F32)
    mid = r.astype(BF16)
    lo = (r - mid.astype(F32)).astype(BF16)
    d = functools.partial(lax.dot_general, dimension_numbers=dims, preferred_element_type=F32)
    return d(hi, ones_b) + (d(mid, ones_b) + d(lo, ones_b))


def _dot_exact_lhs(ones_a, b, dims=_NN):
    hi = b.astype(BF16)
    r = b - hi.astype(F32)
    mid = r.astype(BF16)
    lo = (r - mid.astype(F32)).astype(BF16)
    d = functools.partial(lax.dot_general, dimension_numbers=dims, preferred_element_type=F32)
    return d(ones_a, hi) + (d(ones_a, mid) + d(ones_a, lo))


def _mm_kernel(x_ref, w_ref, o_ref):
    o_ref[...] = _dot(x_ref[...], w_ref[...])


def mm(x, w):
    m, k = x.shape
    n = w.shape[1]
    tm = min(1024, _round_up(m, SUBLANES))
    mp = _round_up(m, tm)
    np_ = _round_up(n, 2 * LANES)
    tn = next(t for t in (1024, 768, 512, 256) if np_ % t == 0)
    if mp != m:
        x = jnp.pad(x, ((0, mp - m), (0, 0)))
    if np_ != n:
        w = jnp.pad(w, ((0, 0), (0, np_ - n)))
    out = pl.pallas_call(
        _mm_kernel,
        grid=(mp // tm, np_ // tn),
        in_specs=[pl.BlockSpec((tm, k), lambda i, j: (i, 0)),
                  pl.BlockSpec((k, tn), lambda i, j: (0, j))],
        out_specs=pl.BlockSpec((tm, tn), lambda i, j: (i, j)),
        out_shape=jax.ShapeDtypeStruct((mp, np_), F32),
        compiler_params=_params("parallel", "arbitrary"),
        name="mm",
    )(x, w)
    return out[:m, :n]


CUMSUM_TILE = 512


def _cumsum_kernel(x_ref, o_ref, carry_ref):
    @pl.when(pl.program_id(0) == 0)
    def _():
        carry_ref[...] = jnp.zeros_like(carry_ref)

    t = CUMSUM_TILE
    row = lax.broadcasted_iota(jnp.int32, (t, t), 0)
    col = lax.broadcasted_iota(jnp.int32, (t, t), 1)
    upper = jnp.where(row <= col, 1.0, 0.0).astype(BF16)
    c = _dot_exact_rhs(x_ref[...], upper) + carry_ref[:, 0:1]
    o_ref[...] = c
    carry_ref[...] = jnp.broadcast_to(c[:, t - 1:t], carry_ref.shape)


def cumsum_rows(x):
    r, s = x.shape
    return pl.pallas_call(
        _cumsum_kernel,
        grid=(s // CUMSUM_TILE,),
        in_specs=[pl.BlockSpec((r, CUMSUM_TILE), lambda i: (0, i))],
        out_specs=pl.BlockSpec((r, CUMSUM_TILE), lambda i: (0, i)),
        out_shape=jax.ShapeDtypeStruct((r, s), F32),
        scratch_shapes=[pltpu.VMEM((r, LANES), F32)],
        compiler_params=_params("arbitrary"),
        name="cumsum_rows",
    )(x)


ATT_TILE = 512


def _fox_prompt_kernel(q_ref, k_ref, v_ref, dq_ref, dk_ref, o_ref, m_ref, l_ref, acc_ref):
    t = ATT_TILE
    qi = pl.program_id(2)
    q = (q_ref[0, 0] * (HEAD_DIM ** -0.5)).astype(BF16)
    dq = dq_ref[0, 0]
    m_ref[...] = jnp.full_like(m_ref, -jnp.inf)
    l_ref[...] = jnp.zeros_like(l_ref)
    acc_ref[...] = jnp.zeros_like(acc_ref)

    def block(kj, masked):
        start = pl.multiple_of(kj * t, t)
        k = k_ref[0, 0, pl.ds(start, t), :]
        v = v_ref[0, 0, pl.ds(start, t), :]
        dk = dk_ref[0, 0, :, pl.ds(start, t)]
        s = _dot(q, k, _NT) + (dq - dk)
        if masked:
            row = lax.broadcasted_iota(jnp.int32, (t, t), 0)
            col = lax.broadcasted_iota(jnp.int32, (t, t), 1)
            s = jnp.where(col <= row, s, -jnp.inf)
        m_old = m_ref[...]
        m_new = jnp.maximum(m_old, jnp.max(s, axis=-1, keepdims=True))
        alpha = jnp.exp(m_old - m_new)
        p = jnp.exp(s - m_new)
        l_ref[...] = alpha * l_ref[...] + jnp.sum(p, axis=-1, keepdims=True)
        acc_ref[...] = alpha * acc_ref[...] + _dot(p, v)
        m_ref[...] = m_new

    def body(kj, carry):
        block(kj, False)
        return carry

    lax.fori_loop(0, qi, body, 0)
    block(qi, True)
    o_ref[0, 0] = acc_ref[...] / l_ref[...]


def fox_prompt(q, k, v, d):
    b, h, s, dh = q.shape
    t = ATT_TILE
    dq = d[..., None]
    dk = d[:, :, None, :]
    full = pl.BlockSpec((1, 1, s, dh), lambda bi, hi, qi: (bi, hi, 0, 0))
    return pl.pallas_call(
        _fox_prompt_kernel,
        grid=(b, h, s // t),
        in_specs=[pl.BlockSpec((1, 1, t, dh), lambda bi, hi, qi: (bi, hi, qi, 0)),
                  full, full,
                  pl.BlockSpec((1, 1, t, 1), lambda bi, hi, qi: (bi, hi, qi, 0)),
                  pl.BlockSpec((1, 1, 1, s), lambda bi, hi, qi: (bi, hi, 0, 0))],
        out_specs=pl.BlockSpec((1, 1, t, dh), lambda bi, hi, qi: (bi, hi, qi, 0)),
        out_shape=jax.ShapeDtypeStruct((b, h, s, dh), F32),
        scratch_shapes=[pltpu.VMEM((t, 1), F32), pltpu.VMEM((t, 1), F32), pltpu.VMEM((t, dh), F32)],
        compiler_params=_params("parallel", "parallel", "arbitrary"),
        name="fox_prompt",
    )(q, k, v, dq, dk)


def _fox_decode_kernel(pt_ref, qcol_ref, knew_ref, vnew_ref, lfnew_ref, k_ref, v_ref, lf_ref, o_ref,
                       qfull_ref, m_ref, l_ref, acc_ref, suf_ref):
    del pt_ref
    p = pl.program_id(1)
    w = W_A
    hrow = lax.broadcasted_iota(jnp.int32, (H_A, w), 0)
    hcol = lax.broadcasted_iota(jnp.int32, (H_A, w), 1) // HEAD_DIM
    expand = jnp.where(hrow == hcol, 1.0, 0.0).astype(BF16)

    @pl.when(p == 0)
    def _():
        r = lax.broadcasted_iota(jnp.int32, (w, w), 0) // HEAD_DIM
        c = lax.broadcasted_iota(jnp.int32, (w, w), 1) // HEAD_DIM
        qfull = jnp.where(r == c, qcol_ref[0] * (HEAD_DIM ** -0.5), 0.0).astype(BF16)
        qfull_ref[...] = qfull
        k_new = jnp.broadcast_to(knew_ref[0], (SUBLANES, w))
        s_new = _dot(k_new, qfull)[0:1]
        m_ref[...] = s_new
        l_ref[...] = jnp.ones_like(l_ref)
        acc_ref[...] = vnew_ref[0]
        lf_new = jnp.broadcast_to(lfnew_ref[0], (SUBLANES, H_A))
        suf_ref[...] = _dot_exact_rhs(lf_new, expand)[0:1]

    kp = k_ref[0]
    vp = v_ref[0]
    lf = _dot_exact_rhs(lf_ref[0], expand)
    r = lax.broadcasted_iota(jnp.int32, (PAGE_SIZE, PAGE_SIZE), 0)
    c = lax.broadcasted_iota(jnp.int32, (PAGE_SIZE, PAGE_SIZE), 1)
    after = jnp.where(c > r, 1.0, 0.0).astype(BF16)
    suf = suf_ref[...]
    bias = _dot_exact_lhs(after, lf) + suf
    s = _dot(kp, qfull_ref[...]) + bias
    m_old = m_ref[...]
    m_new = jnp.maximum(m_old, jnp.max(s, axis=0, keepdims=True))
    alpha = jnp.exp(m_old - m_new)
    pr = jnp.exp(s - m_new)
    l_ref[...] = alpha * l_ref[...] + jnp.sum(pr, axis=0, keepdims=True)
    pv = pr.astype(BF16).astype(F32) * vp.astype(BF16).astype(F32)
    acc_ref[...] = alpha * acc_ref[...] + jnp.sum(pv, axis=0, keepdims=True)
    m_ref[...] = m_new
    suf_ref[...] = suf + jnp.sum(lf, axis=0, keepdims=True)

    @pl.when(p == pl.num_programs(1) - 1)
    def _():
        o_ref[0] = acc_ref[...] / l_ref[...]


def fox_decode(q, k_new, v_new, lf_new, page_table, pool_k, pool_v, pool_lf):
    b, w = q.shape
    n_pages = page_table.shape[1]
    pt = page_table.reshape(-1)

    def page(bi, pi, pt_ref):
        return (pt_ref[bi * n_pages + (n_pages - 1 - pi)], 0, 0)

    per_seq = lambda bi, pi, pt_ref: (bi, 0, 0)
    grid_spec = pltpu.PrefetchScalarGridSpec(
        num_scalar_prefetch=1,
        grid=(b, n_pages),
        in_specs=[pl.BlockSpec((1, w, 1), per_seq),
                  pl.BlockSpec((1, 1, w), per_seq),
                  pl.BlockSpec((1, 1, w), per_seq),
                  pl.BlockSpec((1, 1, H_A), per_seq),
                  pl.BlockSpec((1, PAGE_SIZE, w), page),
                  pl.BlockSpec((1, PAGE_SIZE, w), page),
                  pl.BlockSpec((1, PAGE_SIZE, H_A), page)],
        out_specs=pl.BlockSpec((1, 1, w), per_seq),
        scratch_shapes=[pltpu.VMEM((w, w), BF16), pltpu.VMEM((1, w), F32), pltpu.VMEM((1, w), F32),
                        pltpu.VMEM((1, w), F32), pltpu.VMEM((1, w), F32)],
    )
    out = pl.pallas_call(
        _fox_decode_kernel,
        grid_spec=grid_spec,
        out_shape=jax.ShapeDtypeStruct((b, 1, w), F32),
        compiler_params=_params("parallel", "arbitrary"),
        name="fox_decode",
    )(pt, q[:, :, None], k_new[:, None, :], v_new[:, None, :], lf_new[:, None, :], pool_k, pool_v, pool_lf)
    return out[:, 0, :]


def _chunk_summary(q, kw, v, bvec, c, logw, lgrow_in, vector_decay):
    n = CHUNK
    row = lax.broadcasted_iota(jnp.int32, (n, n), 0)
    col = lax.broadcasted_iota(jnp.int32, (n, n), 1)
    strict = row > col
    incl = row >= col
    lower = jnp.where(incl, 1.0, 0.0).astype(BF16)
    lg = _dot_exact_lhs(lower, logw)
    lgm = lg - logw
    lg_end = lg[n - 1:n, :]
    if vector_decay:
        c_t = c * jnp.exp(lgm)
        b_t = bvec * jnp.exp(-lg)
        k_t = kw * jnp.exp(-lg)
        q_t = q * jnp.exp(lg)
        a_mat = jnp.where(strict, _dot3(c_t, b_t, _NT), 0.0)
        b_mat = jnp.where(strict, _dot3(c_t, k_t, _NT), 0.0)
        mq = jnp.where(incl, _dot3(q_t, k_t, _NT), 0.0)
        nq = jnp.where(incl, _dot3(q_t, b_t, _NT), 0.0)
        kd = kw * jnp.exp(lg_end - lg)
        bd = bvec * jnp.exp(lg_end - lg)
    else:
        upper = jnp.where(row <= col, 1.0, 0.0).astype(BF16)
        lgr = _dot_exact_rhs(lgrow_in, upper)
        lgmr = lgr - lgrow_in
        li = jnp.broadcast_to(lg[:, 0:1], (n, n))
        lmi = jnp.broadcast_to(lgm[:, 0:1], (n, n))
        lj = jnp.broadcast_to(lgr[0:1, :], (n, n))
        lmj = jnp.broadcast_to(lgmr[0:1, :], (n, n))
        g1 = _dot3(c, kw, _NT)
        g2 = _dot3(q, kw, _NT)
        ninf = -jnp.inf
        a_mat = g1 * jnp.exp(jnp.where(strict, lmi - lmj, ninf))
        b_mat = g1 * jnp.exp(jnp.where(strict, lmi - lj, ninf))
        mq = g2 * jnp.exp(jnp.where(incl, li - lj, ninf))
        nq = g2 * jnp.exp(jnp.where(incl, li - lmj, ninf))
        c_t = c * jnp.exp(lgm)
        q_t = q * jnp.exp(lg)
        kd = kw * jnp.exp(lg_end - lg)
        bd = kw * jnp.exp(lg_end - lgm)
    eye = jnp.where(row == col, 1.0, 0.0)
    pw = -a_mat
    t_inv = eye + pw
    for _ in range(int(math.log2(n)) - 1):
        pw = _dot3(pw, pw)
        t_inv = t_inv + _dot3(t_inv, pw)
    w1 = _dot3(t_inv, c_t)
    w2 = _dot3(t_inv, _dot3(b_mat, v))
    qeff = q_t - _dot3(nq, w1)
    oloc = _dot3(mq, v) - _dot3(nq, w2)
    aeff = eye * jnp.exp(lg_end) - _dot3(bd, w1, _TN)
    zloc = _dot3(kd, v, _TN) - _dot3(bd, w2, _TN)
    return qeff, oloc, aeff, zloc


def _chunk_kernel(*refs, vector_decay, chunks):
    if vector_decay:
        q_ref, kw_ref, v_ref, b_ref, c_ref, lw_ref, qeff_ref, oloc_ref, aeff_ref, zloc_ref = refs
        lr_ref = None
    else:
        q_ref, kw_ref, v_ref, c_ref, lw_ref, lr_ref, qeff_ref, oloc_ref, aeff_ref, zloc_ref = refs
        b_ref = None
    for g in range(chunks):
        sl = slice(g * CHUNK, (g + 1) * CHUNK)
        out = _chunk_summary(q_ref[0, sl, :], kw_ref[0, sl, :], v_ref[0, sl, :],
                             b_ref[0, sl, :] if vector_decay else None,
                             c_ref[0, sl, :], lw_ref[0, sl, :],
                             None if vector_decay else lr_ref[0, g],
                             vector_decay)
        qeff_ref[0, sl, :], oloc_ref[0, sl, :], aeff_ref[0, sl, :], zloc_ref[0, sl, :] = out


def _scan_kernel(z0_ref, qeff_ref, oloc_ref, aeff_ref, zloc_ref, o_ref, zfin_ref, z_ref, *, seqs):
    ci = pl.program_id(1)

    @pl.when(ci == 0)
    def _():
        z_ref[...] = z0_ref[...]

    for s in range(seqs):
        z = z_ref[s]
        o_ref[s] = _dot3(qeff_ref[s], z) + oloc_ref[s]
        z_ref[s] = _dot3(aeff_ref[s], z) + zloc_ref[s]

    @pl.when(ci == pl.num_programs(1) - 1)
    def _():
        zfin_ref[...] = z_ref[...]


SCAN_SEQS = 16
CHUNKS_PER_STEP = 4


def dplr_scan(q, kw, v, bvec, c, logw, z0, vector_decay):
    n, t, dh = q.shape
    nc = t // CHUNK
    g = CHUNKS_PER_STEP if nc % CHUNKS_PER_STEP == 0 else 1
    blk = pl.BlockSpec((1, g * CHUNK, dh), lambda i, j: (i, j, 0))
    ins = [q, kw, v] + ([bvec] if vector_decay else []) + [c, logw]
    in_specs = [blk] * len(ins)
    if not vector_decay:
        lrow = jnp.broadcast_to(logw[:, :, 0].reshape(n, nc, 1, CHUNK), (n, nc, SUBLANES, CHUNK))
        ins.append(lrow)
        in_specs.append(pl.BlockSpec((1, g, SUBLANES, CHUNK), lambda i, j: (i, j, 0, 0)))
    shp = jax.ShapeDtypeStruct((n, t, dh), F32)
    qeff, oloc, aeff, zloc = pl.pallas_call(
        functools.partial(_chunk_kernel, vector_decay=vector_decay, chunks=g),
        grid=(n, nc // g),
        in_specs=in_specs,
        out_specs=[blk] * 4,
        out_shape=[shp] * 4,
        compiler_params=_params("parallel", "parallel"),
        name="dplr_chunk_vec" if vector_decay else "dplr_chunk_scalar",
    )(*ins)
    ns = SCAN_SEQS
    sblk = pl.BlockSpec((ns, CHUNK, dh), lambda i, j: (i, j, 0))
    zblk = pl.BlockSpec((ns, dh, dh), lambda i, j: (i, 0, 0))
    o, zfin = pl.pallas_call(
        functools.partial(_scan_kernel, seqs=ns),
        grid=(n // ns, nc),
        in_specs=[zblk, sblk, sblk, sblk, sblk],
        out_specs=[sblk, zblk],
        out_shape=[shp, jax.ShapeDtypeStruct((n, dh, dh), F32)],
        scratch_shapes=[pltpu.VMEM((ns, dh, dh), F32)],
        compiler_params=_params("parallel", "arbitrary"),
        name="dplr_scan",
    )(z0, qeff, oloc, aeff, zloc)
    return o, zfin


def _pad_time(a, t_pad):
    t = a.shape[1]
    if t == t_pad:
        return a
    return jnp.pad(a, ((0, 0), (0, t_pad - t), (0, 0)))


ROUTER_TILE = 512


def _router_kernel(x_ref, w_ref, b_ref, o_ref):
    logits = _dot3(x_ref[...], w_ref[...]) + b_ref[...]
    lane = lax.broadcasted_iota(jnp.int32, logits.shape, 1)
    ninf = -jnp.inf
    big = jnp.int32(LANES)
    g_logit = jnp.where(lane < N_GROUPS, logits, ninf)
    g_max = jnp.max(g_logit, axis=-1, keepdims=True)
    grp = jnp.min(jnp.where(g_logit == g_max, lane, big), axis=-1, keepdims=True)
    g_prob = 1.0 / jnp.sum(jnp.exp(g_logit - g_max), axis=-1, keepdims=True)
    lo = N_GROUPS + grp * EXP_PER_GROUP
    e_sel = jnp.where((lane >= lo) & (lane < lo + EXP_PER_GROUP), logits, ninf)
    v1 = jnp.max(e_sel, axis=-1, keepdims=True)
    i1 = jnp.min(jnp.where(e_sel == v1, lane, big), axis=-1, keepdims=True)
    e_rest = jnp.where(lane == i1, ninf, e_sel)
    v2 = jnp.max(e_rest, axis=-1, keepdims=True)
    i2 = jnp.min(jnp.where(e_rest == v2, lane, big), axis=-1, keepdims=True)
    e21 = jnp.exp(v2 - v1)
    w1 = g_prob / (1.0 + e21)
    w2 = g_prob * e21 / (1.0 + e21)
    out = jnp.where(lane == 0, (i1 - N_GROUPS).astype(F32),
                    jnp.where(lane == 1, (i2 - N_GROUPS).astype(F32),
                              jnp.where(lane == 2, w1, jnp.where(lane == 3, w2, 0.0))))
    o_ref[...] = out


def moe_route(x, w_grp, b_grp, w_exp, b_exp):
    n, d = x.shape
    tm = min(ROUTER_TILE, n)
    pad = LANES - N_GROUPS - N_EXPERTS
    w = jnp.concatenate([w_grp, w_exp, jnp.zeros((d, pad), F32)], axis=1)
    bias = jnp.concatenate([b_grp, b_exp, jnp.zeros((pad,), F32)])[None, :]
    out = pl.pallas_call(
        _router_kernel,
        grid=(n // tm,),
        in_specs=[pl.BlockSpec((tm, d), lambda i: (i, 0)),
                  pl.BlockSpec((d, LANES), lambda i: (0, 0)),
                  pl.BlockSpec((1, LANES), lambda i: (0, 0))],
        out_specs=pl.BlockSpec((tm, LANES), lambda i: (i, 0)),
        out_shape=jax.ShapeDtypeStruct((n, LANES), F32),
        compiler_params=_params("parallel"),
        name="moe_router",
    )(x, w, bias)
    return out[:, 0:2].astype(jnp.int32), out[:, 2:4]


def _expert_kernel(te_ref, x_ref, wg_ref, wu_ref, wd_ref, o_ref):
    del te_ref
    x = x_ref[...].astype(BF16)
    gate = _dot(x, wg_ref[0])
    up = _dot(x, wu_ref[0])
    hid = jax.nn.silu(gate) * up
    o_ref[...] = _dot(hid, wd_ref[0])


def moe_experts(x_sorted, tile_expert, w_gate, w_up, w_down, tm):
    p, d = x_sorted.shape
    f = w_gate.shape[2]
    grid_spec = pltpu.PrefetchScalarGridSpec(
        num_scalar_prefetch=1,
        grid=(p // tm,),
        in_specs=[pl.BlockSpec((tm, d), lambda i, te: (i, 0)),
                  pl.BlockSpec((1, d, f), lambda i, te: (te[i], 0, 0)),
                  pl.BlockSpec((1, d, f), lambda i, te: (te[i], 0, 0)),
                  pl.BlockSpec((1, f, d), lambda i, te: (te[i], 0, 0))],
        out_specs=pl.BlockSpec((tm, d), lambda i, te: (i, 0)),
    )
    return pl.pallas_call(
        _expert_kernel,
        grid_spec=grid_spec,
        out_shape=jax.ShapeDtypeStruct((p, d), F32),
        compiler_params=_params("arbitrary"),
        name="moe_experts",
    )(tile_expert, x_sorted, w_gate, w_up, w_down)


def hier_moe(x, w_grp, b_grp, w_exp, b_exp, w_gate, w_up, w_down):
    n, d = x.shape
    eid, wts = moe_route(x, w_grp, b_grp, w_exp, b_exp)
    tm = 256 if n >= 4096 else 32
    n_assign = 2 * n
    p = n_assign + N_EXPERTS * tm
    e_flat = eid.reshape(-1)
    onehot = (e_flat[:, None] == jnp.arange(N_EXPERTS, dtype=jnp.int32)[None, :]).astype(jnp.int32)
    counts = jnp.sum(onehot, axis=0)
    rank = jnp.sum((jnp.cumsum(onehot, axis=0) - onehot) * onehot, axis=1)
    padded = (counts + tm - 1) // tm * tm
    seg_end = jnp.cumsum(padded)
    seg_start = seg_end - padded
    dest = seg_start[e_flat] + rank
    src_token = jnp.full((p,), n, jnp.int32).at[dest].set(jnp.arange(n_assign, dtype=jnp.int32) // 2)
    x_pad = jnp.concatenate([x, jnp.zeros((1, d), x.dtype)], axis=0)
    x_sorted = x_pad[src_token]
    tile_start = jnp.arange(p // tm, dtype=jnp.int32) * tm
    tile_expert = jnp.minimum(jnp.searchsorted(seg_end, tile_start, side="right"), N_EXPERTS - 1).astype(jnp.int32)
    out_sorted = moe_experts(x_sorted, tile_expert, w_gate, w_up, w_down, tm)
    picked = out_sorted[dest].reshape(n, 2, d)
    return jnp.sum(picked * wts[:, :, None], axis=1)


def _rmsnorm(x, g):
    return x * lax.rsqrt(jnp.mean(x * x, axis=-1, keepdims=True) + RMS_EPS) * g


def _l2norm(x):
    return x * lax.rsqrt(jnp.sum(x * x, axis=-1, keepdims=True) + RMS_EPS)


def _heads(t):
    return t.reshape(t.shape[:-1] + (-1, HEAD_DIM))


def _seq_major(t):
    b, s, h, dh = t.shape
    return t.transpose(0, 2, 1, 3).reshape(b * h, s, dh)


def _ab_mixer(h, p, past):
    b, t, d = h.shape
    z = mm(h.reshape(b * t, d), p["w_in"]).reshape(b, t, -1)
    qa, ka, va, fa, qb, kb, vb, gb, beta_in, alpha_in = jnp.split(z, IN_OFFSETS, axis=-1)
    lf = jax.nn.log_sigmoid(fa + p["b_f"])
    qkv_b = jnp.concatenate([qb, kb, vb], axis=-1)
    if past is None:
        d_cum = cumsum_rows(lf.transpose(0, 2, 1).reshape(b * H_A, t)).reshape(b, H_A, t)
        to_heads = lambda a: _heads(a).transpose(0, 2, 1, 3)
        o_a = fox_prompt(to_heads(qa), to_heads(ka), to_heads(va), d_cum)
        o_a = o_a.transpose(0, 2, 1, 3).reshape(b, t, W_A)
        conv_in = jnp.pad(qkv_b, ((0, 0), (CONV_W - 1, 0), (0, 0)))
        s0 = jnp.zeros((b * H_B, HEAD_DIM, HEAD_DIM), F32)
    else:
        page_table, pool_k, pool_v, pool_lf, ssm, conv_buf = past
        n_pool = pool_k.shape[0]
        o_a = fox_decode(qa[:, 0], ka[:, 0], va[:, 0], lf[:, 0], page_table,
                         pool_k.reshape(n_pool, PAGE_SIZE, W_A), pool_v.reshape(n_pool, PAGE_SIZE, W_A), pool_lf)
        o_a = o_a[:, None, :]
        conv_in = jnp.concatenate([conv_buf, qkv_b], axis=1)
        s0 = ssm.reshape(b * H_B, HEAD_DIM, HEAD_DIM)
    conv_out = jax.nn.silu(sum(conv_in[:, i:i + t] * p["conv_w"][i] for i in range(CONV_W)))
    new_conv = conv_in[:, -(CONV_W - 1):]
    qc, kc, vc = jnp.split(conv_out, 3, axis=-1)
    qn = _l2norm(_heads(qc)) * (HEAD_DIM ** -0.5)
    kn = _l2norm(_heads(kc))
    vn = _heads(vc)
    beta = jax.nn.sigmoid(beta_in)
    log_g = -jnp.exp(p["a_log"]) * jax.nn.softplus(alpha_in + p["dt_bias"])
    t_pad = _round_up(t, CHUNK)
    sm = lambda a: _pad_time(_seq_major(a), t_pad)
    logw = jnp.broadcast_to(log_g[..., None], (b, t, H_B, HEAD_DIM))
    o_b, s_new = dplr_scan(sm(qn), sm(kn * beta[..., None]), sm(vn), None, sm(kn), sm(logw), s0, False)
    o_b = o_b[:, :t].reshape(b, H_B, t, HEAD_DIM).transpose(0, 2, 1, 3)
    o_b = _rmsnorm(o_b, p["o_norm"]) * jax.nn.silu(_heads(gb))
    o = jnp.concatenate([o_a, o_b.reshape(b, t, W_B)], axis=-1)
    out = mm(o.reshape(b * t, W_A + W_B), p["w_out"]).reshape(b, t, d)
    return out, (_heads(ka), _heads(va), lf, s_new.reshape(b, H_B, HEAD_DIM, HEAD_DIM), new_conv)


def _rwkv7_mixer(h, prev, s0, p):
    b, t, d = h.shape
    h_prev = jnp.concatenate([prev[:, None, :], h[:, :-1]], axis=1)
    xx = h_prev - h
    xr, xw, xk, xv, xa, xg = (h + xx * p["mu"][i] for i in range(6))
    flat = lambda a: a.reshape(b * t, -1)
    r = mm(flat(xr), p["w_rkv"][0])
    k = mm(flat(xk), p["w_rkv"][1])
    v = mm(flat(xv), p["w_rkv"][2])
    w_log = -jax.nn.softplus(-(p["w0"] + mm(jnp.tanh(mm(flat(xw), p["w1"])), p["w2"]))) - 0.5
    log_decay = -jnp.exp(w_log)
    a = jax.nn.sigmoid(p["a0"] + mm(mm(flat(xa), p["a1"]), p["a2"]))
    g = mm(jax.nn.sigmoid(mm(flat(xg), p["g1"])), p["g2"])
    kk = _l2norm(_heads(k * p["k_k"]))
    k = k * (1 + (a - 1) * p["k_a"])
    r_h, k_h, v_h, a_h = _heads(r), _heads(k), _heads(v), _heads(a)
    t_pad = _round_up(t, CHUNK)
    sm = lambda z: _pad_time(_seq_major(z.reshape(b, t, H_C, HEAD_DIM)), t_pad)
    z0 = s0.reshape(b * H_C, HEAD_DIM, HEAD_DIM).transpose(0, 2, 1)
    y, zfin = dplr_scan(sm(r_h), sm(k_h), sm(v_h), sm(kk * a_h), sm(kk), sm(log_decay), z0, True)
    s_fin = zfin.transpose(0, 2, 1).reshape(b, H_C, HEAD_DIM, HEAD_DIM)
    y = y[:, :t].reshape(b, H_C, t, HEAD_DIM).transpose(0, 2, 1, 3).reshape(b * t, H_C, HEAD_DIM)
    mean = jnp.mean(y, axis=-1, keepdims=True)
    var = jnp.mean(jnp.square(y - mean), axis=-1, keepdims=True)
    y = ((y - mean) * lax.rsqrt(var + GN_EPS)).reshape(b * t, d) * p["gn_w"] + p["gn_b"]
    bonus = jnp.sum(r_h * k_h * p["r_k"], axis=-1, keepdims=True) * v_h
    y = y + bonus.reshape(b * t, d)
    out = mm(y * g, p["w_out"]).reshape(b, t, d)
    return out, s_fin, h[:, -1]


def _trunk(x, c, past, P):
    depth = P["w_mod"].shape[0]
    ab_states, rw_states = [], []
    bsz, t, d = x.shape
    for layer in range(depth):
        mod = mm(jax.nn.silu(c), P["w_mod"][layer]) + P["b_mod"][layer]
        sh1, sc1, g1, sh2, sc2, g2 = jnp.split(mod[:, None, :], 6, axis=-1)
        h = _rmsnorm(x, P["norm_mix"][layer]) * (1 + sc1) + sh1
        if layer % 2 == 0:
            i = layer // 2
            lp = {name: arr[i] for name, arr in P["ab"].items()}
            lpast = None if past is None else (past["page_table"], past["cache_k"][i], past["cache_v"][i],
                                               past["cache_lf"][i], past["ssm"][i], past["conv"][i])
            out, st = _ab_mixer(h, lp, lpast)
            ab_states.append(st)
        else:
            j = layer // 2
            lp = {name: arr[j] for name, arr in P["rw"].items()}
            if past is None:
                prev = jnp.zeros((bsz, d), x.dtype)
                s0 = jnp.zeros((bsz, H_C, HEAD_DIM, HEAD_DIM), F32)
            else:
                prev, s0 = past["shift"][j], past["wkv"][j]
            out, s_fin, last = _rwkv7_mixer(h, prev, s0, lp)
            rw_states.append((s_fin, last))
        x = x + g1 * out
        h = _rmsnorm(x, P["norm_ffn"][layer]) * (1 + sc2) + sh2
        mp = {name: arr[layer] for name, arr in P["moe"].items()}
        y = hier_moe(h.reshape(bsz * t, d), mp["w_grp"], mp["b_grp"], mp["w_exp"], mp["b_exp"],
                     mp["w_gate"], mp["w_up"], mp["w_down"])
        x = x + g2 * y.reshape(bsz, t, d)
    return _rmsnorm(x, P["norm_final"]), ab_states, rw_states


def _stack(states):
    return tuple(jnp.stack([s[m] for s in states]) for m in range(len(states[0])))


def kernel(x_prompt, x_sample, c_prompt, c_sample, page_table, cache_k_a, cache_v_a, cache_lf_a, state_ssm_b, state_conv_b, state_wkv_c, state_shift_c, w_mod, b_mod, norm_mix, norm_ffn, norm_final, ab_w_in, ab_b_f, ab_conv_w, ab_a_log, ab_dt_bias, ab_o_norm, ab_w_out, rw_mu, rw_w_rkv, rw_w0, rw_w1, rw_w2, rw_a0, rw_a1, rw_a2, rw_g1, rw_g2, rw_k_k, rw_k_a, rw_r_k, rw_gn_w, rw_gn_b, rw_w_out, moe_w_grp, moe_b_grp, moe_w_exp, moe_b_exp, moe_w_gate, moe_w_up, moe_w_down):
    P = {
        "w_mod": w_mod, "b_mod": b_mod, "norm_mix": norm_mix, "norm_ffn": norm_ffn, "norm_final": norm_final,
        "ab": {"w_in": ab_w_in, "b_f": ab_b_f, "conv_w": ab_conv_w, "a_log": ab_a_log,
               "dt_bias": ab_dt_bias, "o_norm": ab_o_norm, "w_out": ab_w_out},
        "rw": {"mu": rw_mu, "w_rkv": rw_w_rkv, "w0": rw_w0, "w1": rw_w1, "w2": rw_w2,
               "a0": rw_a0, "a1": rw_a1, "a2": rw_a2, "g1": rw_g1, "g2": rw_g2,
               "k_k": rw_k_k, "k_a": rw_k_a, "r_k": rw_r_k, "gn_w": rw_gn_w, "gn_b": rw_gn_b,
               "w_out": rw_w_out},
        "moe": {"w_grp": moe_w_grp, "b_grp": moe_b_grp, "w_exp": moe_w_exp, "b_exp": moe_b_exp,
                "w_gate": moe_w_gate, "w_up": moe_w_up, "w_down": moe_w_down},
    }
    y_prompt, ab_p, rw_p = _trunk(x_prompt, c_prompt, None, P)
    past = {"page_table": page_table, "cache_k": cache_k_a, "cache_v": cache_v_a, "cache_lf": cache_lf_a,
            "ssm": state_ssm_b, "conv": state_conv_b, "wkv": state_wkv_c, "shift": state_shift_c}
    y_sample, ab_s, rw_s = _trunk(x_sample, c_sample, past, P)
    return (y_prompt, y_sample) + _stack(ab_p) + _stack(rw_p) + _stack(ab_s) + _stack(rw_s)
```

```python
import functools
import math

import jax
import jax.numpy as jnp
import numpy as np
from jax import lax
from jax.experimental import pallas as pl
from jax.experimental.pallas import tpu as pltpu

F32 = jnp.float32
BF16 = jnp.bfloat16

D_MODEL = 1024
HEAD_DIM = 64
H_A = 8
H_B = 8
W_A = H_A * HEAD_DIM
W_B = H_B * HEAD_DIM
CONV_W = 4
H_C = D_MODEL // HEAD_DIM
N_GROUPS = 4
EXP_PER_GROUP = 8
N_EXPERTS = N_GROUPS * EXP_PER_GROUP
D_EXPERT = D_MODEL // 4
PAGE_SIZE = 128
IN_SIZES = (W_A, W_A, W_A, H_A, W_B, W_B, W_B, W_B, H_B, H_B)
IN_OFFSETS = tuple(int(v) for v in np.cumsum(IN_SIZES)[:-1])
RMS_EPS = 1e-6
GN_EPS = 64e-5

LANES = 128
SUBLANES = 8
CHUNK = 64
VMEM_LIMIT = 48 * 1024 * 1024

_NN = (((1,), (0,)), ((), ()))
_NT = (((1,), (1,)), ((), ()))
_TN = (((0,), (0,)), ((), ()))


def _round_up(n, m):
    return (n + m - 1) // m * m


def _params(*sem):
    return pltpu.CompilerParams(dimension_semantics=sem, vmem_limit_bytes=VMEM_LIMIT)


def _dot(a, b, dims=_NN):
    return lax.dot_general(a.astype(BF16), b.astype(BF16), dims, preferred_element_type=F32)


def _split2(a):
    hi = a.astype(BF16)
    lo = (a - hi.astype(F32)).astype(BF16)
    return hi, lo


def _dot3(a, b, dims=_NN):
    ah, al = _split2(a)
    bh, bl = _split2(b)
    d = functools.partial(lax.dot_general, dimension_numbers=dims, preferred_element_type=F32)
    return d(ah, bh) + (d(ah, bl) + d(al, bh))


def _dot3_rows(a, b, dims=_NN):
    m = a.shape[0]
    ah, al = _split2(a)
    bh, bl = _split2(b)
    d = functools.partial(lax.dot_general, dimension_numbers=dims, preferred_element_type=F32)
    top = d(jnp.concatenate([ah, al], axis=0), bh)
    return top[:m] + top[m:] + d(ah, bl)


def _dot_exact_rhs(a, ones_b, dims=_NN):
    hi = a.astype(BF16)
    r = a - hi.astype(F32)
    mid = r.astype(BF16)
    lo = (r - mid.astype(F32)).astype(BF16)
    d = functools.partial(lax.dot_general, dimension_numbers=dims, preferred_element_type=F32)
    return d(hi, ones_b) + (d(mid, ones_b) + d(lo, ones_b))


def _dot_exact_lhs(ones_a, b, dims=_NN):
    hi = b.astype(BF16)
    r = b - hi.astype(F32)
    mid = r.astype(BF16)
    lo = (r - mid.astype(F32)).astype(BF16)
    d = functools.partial(lax.dot_general, dimension_numbers=dims, preferred_element_type=F32)
    return d(ones_a, hi) + (d(ones_a, mid) + d(ones_a, lo))


def _mm_kernel(x_ref, w_ref, o_ref):
    o_ref[...] = _dot(x_ref[...], w_ref[...])


def mm(x, w):
    m, k = x.shape
    n = w.shape[1]
    tm = min(1024, _round_up(m, SUBLANES))
    mp = _round_up(m, tm)
    np_ = _round_up(n, 2 * LANES)
    tn = next(t for t in (1024, 768, 512, 256) if np_ % t == 0)
    if mp != m:
        x = jnp.pad(x, ((0, mp - m), (0, 0)))
    if np_ != n:
        w = jnp.pad(w, ((0, 0), (0, np_ - n)))
    out = pl.pallas_call(
        _mm_kernel,
        grid=(mp // tm, np_ // tn),
        in_specs=[pl.BlockSpec((tm, k), lambda i, j: (i, 0)),
                  pl.BlockSpec((k, tn), lambda i, j: (0, j))],
        out_specs=pl.BlockSpec((tm, tn), lambda i, j: (i, j)),
        out_shape=jax.ShapeDtypeStruct((mp, np_), F32),
        compiler_params=_params("parallel", "arbitrary"),
        name="mm",
    )(x, w)
    return out[:m, :n]


CUMSUM_TILE = 512


def _cumsum_kernel(x_ref, o_ref, carry_ref):
    @pl.when(pl.program_id(0) == 0)
    def _():
        carry_ref[...] = jnp.zeros_like(carry_ref)

    t = CUMSUM_TILE
    row = lax.broadcasted_iota(jnp.int32, (t, t), 0)
    col = lax.broadcasted_iota(jnp.int32, (t, t), 1)
    upper = jnp.where(row <= col, 1.0, 0.0).astype(BF16)
    c = _dot_exact_rhs(x_ref[...], upper) + carry_ref[:, 0:1]
    o_ref[...] = c
    carry_ref[...] = jnp.broadcast_to(c[:, t - 1:t], carry_ref.shape)


def cumsum_rows(x):
    r, s = x.shape
    return pl.pallas_call(
        _cumsum_kernel,
        grid=(s // CUMSUM_TILE,),
        in_specs=[pl.BlockSpec((r, CUMSUM_TILE), lambda i: (0, i))],
        out_specs=pl.BlockSpec((r, CUMSUM_TILE), lambda i: (0, i)),
        out_shape=jax.ShapeDtypeStruct((r, s), F32),
        scratch_shapes=[pltpu.VMEM((r, LANES), F32)],
        compiler_params=_params("arbitrary"),
        name="cumsum_rows",
    )(x)


ATT_TILE = 512


def _fox_prompt_kernel(q_ref, k_ref, v_ref, dq_ref, dk_ref, o_ref, m_ref, l_ref, acc_ref):
    t = ATT_TILE
    qi = pl.program_id(2)
    q = (q_ref[0, 0] * (HEAD_DIM ** -0.5)).astype(BF16)
    dq = dq_ref[0, 0]
    m_ref[...] = jnp.full_like(m_ref, -jnp.inf)
    l_ref[...] = jnp.zeros_like(l_ref)
    acc_ref[...] = jnp.zeros_like(acc_ref)

    def block(kj, masked):
        start = pl.multiple_of(kj * t, t)
        k = k_ref[0, 0, pl.ds(start, t), :]
        v = v_ref[0, 0, pl.ds(start, t), :]
        dk = dk_ref[0, 0, :, pl.ds(start, t)]
        s = _dot(q, k, _NT) + (dq - dk)
        if masked:
            row = lax.broadcasted_iota(jnp.int32, (t, t), 0)
            col = lax.broadcasted_iota(jnp.int32, (t, t), 1)
            s = jnp.where(col <= row, s, -jnp.inf)
        m_old = m_ref[...]
        m_new = jnp.maximum(m_old, jnp.max(s, axis=-1, keepdims=True))
        alpha = jnp.exp(m_old - m_new)
        p = jnp.exp(s - m_new)
        l_ref[...] = alpha * l_ref[...] + jnp.sum(p, axis=-1, keepdims=True)
        acc_ref[...] = alpha * acc_ref[...] + _dot(p, v)
        m_ref[...] = m_new

    def body(kj, carry):
        block(kj, False)
        return carry

    lax.fori_loop(0, qi, body, 0)
    block(qi, True)
    o_ref[0, 0] = acc_ref[...] / l_ref[...]


def fox_prompt(q, k, v, d):
    b, h, s, dh = q.shape
    t = ATT_TILE
    dq = d[..., None]
    dk = d[:, :, None, :]
    full = pl.BlockSpec((1, 1, s, dh), lambda bi, hi, qi: (bi, hi, 0, 0))
    return pl.pallas_call(
        _fox_prompt_kernel,
        grid=(b, h, s // t),
        in_specs=[pl.BlockSpec((1, 1, t, dh), lambda bi, hi, qi: (bi, hi, qi, 0)),
                  full, full,
                  pl.BlockSpec((1, 1, t, 1), lambda bi, hi, qi: (bi, hi, qi, 0)),
                  pl.BlockSpec((1, 1, 1, s), lambda bi, hi, qi: (bi, hi, 0, 0))],
        out_specs=pl.BlockSpec((1, 1, t, dh), lambda bi, hi, qi: (bi, hi, qi, 0)),
        out_shape=jax.ShapeDtypeStruct((b, h, s, dh), F32),
        scratch_shapes=[pltpu.VMEM((t, 1), F32), pltpu.VMEM((t, 1), F32), pltpu.VMEM((t, dh), F32)],
        compiler_params=_params("parallel", "parallel", "arbitrary"),
        name="fox_prompt",
    )(q, k, v, dq, dk)


DEC_PAGES = 8


def _fox_decode_kernel(pt_ref, qb_ref, knew_ref, vnew_ref, lfnew_ref, *rest):
    del pt_ref
    npg = DEC_PAGES
    k_refs, v_refs, lf_refs = rest[:npg], rest[npg:2 * npg], rest[2 * npg:3 * npg]
    o_ref, m_ref, l_ref, acc_ref, suf_ref = rest[3 * npg:]
    p = pl.program_id(1)
    qb = qb_ref[0]

    @pl.when(p == 0)
    def _():
        m_ref[...] = jnp.sum(knew_ref[0] * qb, axis=1)
        l_ref[...] = jnp.ones_like(l_ref)
        lane = lax.broadcasted_iota(jnp.int32, acc_ref.shape, 2)
        acc_ref[...] = jnp.where(lane == 0, vnew_ref[0], 0.0)
        suf_ref[...] = lfnew_ref[0]

    r = lax.broadcasted_iota(jnp.int32, (PAGE_SIZE, PAGE_SIZE), 0)
    c = lax.broadcasted_iota(jnp.int32, (PAGE_SIZE, PAGE_SIZE), 1)
    after = jnp.where(r > c, 1.0, 0.0).astype(BF16)
    for j in range(npg):
        kt = k_refs[j][0]
        vt = v_refs[j][0]
        lf = lf_refs[j][0]
        suf = suf_ref[...]
        s = jnp.sum(kt * qb, axis=1) + _dot_exact_rhs(lf, after) + suf
        m_old = m_ref[...]
        m_new = jnp.maximum(m_old, jnp.max(s, axis=-1, keepdims=True))
        alpha = jnp.exp(m_old - m_new)
        pr = jnp.exp(s - m_new)
        l_ref[...] = alpha * l_ref[...] + jnp.sum(pr, axis=-1, keepdims=True)
        acc_ref[...] = alpha[:, None, :] * acc_ref[...] + pr[:, None, :] * vt
        m_ref[...] = m_new
        suf_ref[...] = suf + jnp.sum(lf, axis=-1, keepdims=True)

    @pl.when(p == pl.num_programs(1) - 1)
    def _():
        o_ref[0] = jnp.sum(acc_ref[...], axis=-1) / l_ref[:, 0:1]


def fox_decode(q, k_new, v_new, lf_new, page_table, pool_kt, pool_vt, pool_lft):
    b, h, dh = q.shape
    n_pages = page_table.shape[1]
    npg = DEC_PAGES
    pt = page_table.reshape(-1)
    rep = lambda a: jnp.broadcast_to(a[..., None], a.shape + (PAGE_SIZE,))

    def page_map(j, nd):
        def f(bi, pi, pt_ref):
            return (pt_ref[bi * n_pages + (n_pages - 1 - (pi * npg + j))],) + (0,) * nd
        return f

    seq4 = lambda bi, pi, pt_ref: (bi, 0, 0, 0)
    seq3 = lambda bi, pi, pt_ref: (bi, 0, 0)
    vec = pl.BlockSpec((1, h, dh, PAGE_SIZE), seq4)
    grid_spec = pltpu.PrefetchScalarGridSpec(
        num_scalar_prefetch=1,
        grid=(b, n_pages // npg),
        in_specs=([vec, vec, vec, pl.BlockSpec((1, h, PAGE_SIZE), seq3)]
                  + [pl.BlockSpec((1, h, dh, PAGE_SIZE), page_map(j, 3)) for j in range(npg)]
                  + [pl.BlockSpec((1, h, dh, PAGE_SIZE), page_map(j, 3)) for j in range(npg)]
                  + [pl.BlockSpec((1, h, PAGE_SIZE), page_map(j, 2)) for j in range(npg)]),
        out_specs=pl.BlockSpec((1, h, dh), seq3),
        scratch_shapes=[pltpu.VMEM((h, PAGE_SIZE), F32), pltpu.VMEM((h, PAGE_SIZE), F32),
                        pltpu.VMEM((h, dh, PAGE_SIZE), F32), pltpu.VMEM((h, PAGE_SIZE), F32)],
    )
    return pl.pallas_call(
        _fox_decode_kernel,
        grid_spec=grid_spec,
        out_shape=jax.ShapeDtypeStruct((b, h, dh), F32),
        compiler_params=_params("parallel", "arbitrary"),
        name="fox_decode",
    )(pt, rep(q * (HEAD_DIM ** -0.5)), rep(k_new), rep(v_new), rep(lf_new),
      *([pool_kt] * npg), *([pool_vt] * npg), *([pool_lft] * npg))


def _chunk_summaries(qs, kws, vs, bs, cs, lws, lrs, vector_decay):
    n = CHUNK
    dk = qs[0].shape[1]
    idx = range(len(qs))
    row = lax.broadcasted_iota(jnp.int32, (n, n), 0)
    col = lax.broadcasted_iota(jnp.int32, (n, n), 1)
    strict = row > col
    incl = row >= col
    lower = jnp.where(incl, 1.0, 0.0).astype(BF16)
    eye = jnp.where(row == col, 1.0, 0.0)
    lg = [_dot_exact_lhs(lower, lws[g]) for g in idx]
    lgm = [lg[g] - lws[g] for g in idx]
    lg_end = [lg[g][n - 1:n, :] for g in idx]
    c_t = [cs[g] * jnp.exp(lgm[g]) for g in idx]
    q_t = [qs[g] * jnp.exp(lg[g]) for g in idx]
    if vector_decay:
        e_neg = [jnp.exp(-lg[g]) for g in idx]
        e_end = [jnp.exp(lg_end[g] - lg[g]) for g in idx]
        kd = [kws[g] * e_end[g] for g in idx]
        bd = [bs[g] * e_end[g] for g in idx]
        rhs = [jnp.concatenate([bs[g] * e_neg[g], kws[g] * e_neg[g]], axis=0) for g in idx]
        top = [_dot3_rows(c_t[g], rhs[g], _NT) for g in idx]
        bot = [_dot(q_t[g], rhs[g], _NT) for g in idx]
        a_mat = [jnp.where(strict, top[g][:, :n], 0.0) for g in idx]
        b_mat = [jnp.where(strict, top[g][:, n:], 0.0) for g in idx]
        nq = [jnp.where(incl, bot[g][:, :n], 0.0) for g in idx]
        mq = [jnp.where(incl, bot[g][:, n:], 0.0) for g in idx]
    else:
        upper = jnp.where(row <= col, 1.0, 0.0).astype(BF16)
        ninf = -jnp.inf
        kd = [kws[g] * jnp.exp(lg_end[g] - lg[g]) for g in idx]
        bd = [kws[g] * jnp.exp(lg_end[g] - lgm[g]) for g in idx]
        g1 = [_dot3_rows(cs[g], kws[g], _NT) for g in idx]
        g2 = [_dot(qs[g], kws[g], _NT) for g in idx]
        a_mat, b_mat, nq, mq = [], [], [], []
        for g in idx:
            lgr = _dot_exact_rhs(lrs[g], upper)
            lgmr = lgr - lrs[g]
            li = jnp.broadcast_to(lg[g][:, 0:1], (n, n))
            lmi = jnp.broadcast_to(lgm[g][:, 0:1], (n, n))
            lj = jnp.broadcast_to(lgr[0:1, :], (n, n))
            lmj = jnp.broadcast_to(lgmr[0:1, :], (n, n))
            a_mat.append(g1[g] * jnp.exp(jnp.where(strict, lmi - lmj, ninf)))
            b_mat.append(g1[g] * jnp.exp(jnp.where(strict, lmi - lj, ninf)))
            mq.append(g2[g] * jnp.exp(jnp.where(incl, li - lj, ninf)))
            nq.append(g2[g] * jnp.exp(jnp.where(incl, li - lmj, ninf)))
    pw = [-a_mat[g] for g in idx]
    t_inv = [eye + pw[g] for g in idx]
    pw = [_dot3_rows(pw[g], pw[g]) for g in idx]
    for _ in range(int(math.log2(n)) - 2):
        res = [_dot3_rows(pw[g], jnp.concatenate([pw[g], t_inv[g]], axis=1)) for g in idx]
        pw = [res[g][:, :n] for g in idx]
        t_inv = [t_inv[g] + res[g][:, n:] for g in idx]
    t_inv = [t_inv[g] + _dot3_rows(pw[g], t_inv[g]) for g in idx]
    bmv = [_dot3_rows(b_mat[g], vs[g]) for g in idx]
    w12 = [_dot3_rows(t_inv[g], jnp.concatenate([c_t[g], bmv[g]], axis=1)) for g in idx]
    zero = jnp.zeros((n, dk), F32)
    rmat = [jnp.concatenate([w12[g], jnp.concatenate([zero, -vs[g]], axis=1)], axis=0) for g in idx]
    res6 = [_dot(jnp.concatenate([nq[g], mq[g]], axis=1), rmat[g]) for g in idx]
    res7 = [_dot3(jnp.concatenate([bd[g], kd[g]], axis=0), rmat[g], _TN) for g in idx]
    out = []
    for g in idx:
        qeff = q_t[g] - res6[g][:, :dk]
        oloc = -res6[g][:, dk:]
        aeff = eye * jnp.exp(lg_end[g]) - res7[g][:, :dk]
        zloc = -res7[g][:, dk:]
        out.append((qeff, oloc, aeff, zloc))
    return out


def _chunk_kernel(*refs, vector_decay, chunks):
    if vector_decay:
        q_ref, kw_ref, v_ref, b_ref, c_ref, lw_ref, qeff_ref, oloc_ref, aeff_ref, zloc_ref = refs
        lr_ref = None
    else:
        q_ref, kw_ref, v_ref, c_ref, lw_ref, lr_ref, qeff_ref, oloc_ref, aeff_ref, zloc_ref = refs
        b_ref = None
    sls = [slice(g * CHUNK, (g + 1) * CHUNK) for g in range(chunks)]
    take = lambda ref: [ref[0, sl, :] for sl in sls]
    outs = _chunk_summaries(take(q_ref), take(kw_ref), take(v_ref),
                            take(b_ref) if vector_decay else None, take(c_ref), take(lw_ref),
                            None if vector_decay else [lr_ref[0, g] for g in range(chunks)],
                            vector_decay)
    for sl, out in zip(sls, outs):
        qeff_ref[0, sl, :], oloc_ref[0, sl, :], aeff_ref[0, sl, :], zloc_ref[0, sl, :] = out


def _scan_kernel(z0_ref, qeff_ref, oloc_ref, aeff_ref, zloc_ref, o_ref, zfin_ref, z_ref, *, seqs):
    ci = pl.program_id(1)

    @pl.when(ci == 0)
    def _():
        z_ref[...] = z0_ref[...]

    zs = [z_ref[s] for s in range(seqs)]
    for s in range(seqs):
        o_ref[s] = _dot3_rows(qeff_ref[s], zs[s]) + oloc_ref[s]
    for s in range(seqs):
        z_ref[s] = _dot3_rows(aeff_ref[s], zs[s]) + zloc_ref[s]

    @pl.when(ci == pl.num_programs(1) - 1)
    def _():
        zfin_ref[...] = z_ref[...]


SCAN_SEQS = 16
CHUNKS_PER_STEP = 16


def dplr_scan(q, kw, v, bvec, c, logw, z0, vector_decay):
    n, t, dh = q.shape
    nc = t // CHUNK
    g = CHUNKS_PER_STEP
    assert t % (g * CHUNK) == 0 and n % SCAN_SEQS == 0
    blk = pl.BlockSpec((1, g * CHUNK, dh), lambda i, j: (i, j, 0))
    ins = [q, kw, v] + ([bvec] if vector_decay else []) + [c, logw]
    in_specs = [blk] * len(ins)
    if not vector_decay:
        lrow = jnp.broadcast_to(logw[:, :, 0].reshape(n, nc, 1, CHUNK), (n, nc, SUBLANES, CHUNK))
        ins.append(lrow)
        in_specs.append(pl.BlockSpec((1, g, SUBLANES, CHUNK), lambda i, j: (i, j, 0, 0)))
    shp = jax.ShapeDtypeStruct((n, t, dh), F32)
    qeff, oloc, aeff, zloc = pl.pallas_call(
        functools.partial(_chunk_kernel, vector_decay=vector_decay, chunks=g),
        grid=(n, nc // g),
        in_specs=in_specs,
        out_specs=[blk] * 4,
        out_shape=[shp] * 4,
        compiler_params=_params("parallel", "parallel"),
        name="dplr_chunk_vec" if vector_decay else "dplr_chunk_scalar",
    )(*ins)
    ns = SCAN_SEQS
    sblk = pl.BlockSpec((ns, CHUNK, dh), lambda i, j: (i, j, 0))
    zblk = pl.BlockSpec((ns, dh, dh), lambda i, j: (i, 0, 0))
    o, zfin = pl.pallas_call(
        functools.partial(_scan_kernel, seqs=ns),
        grid=(n // ns, nc),
        in_specs=[zblk, sblk, sblk, sblk, sblk],
        out_specs=[sblk, zblk],
        out_shape=[shp, jax.ShapeDtypeStruct((n, dh, dh), F32)],
        scratch_shapes=[pltpu.VMEM((ns, dh, dh), F32)],
        compiler_params=_params("parallel", "arbitrary"),
        name="dplr_scan",
    )(z0, qeff, oloc, aeff, zloc)
    return o, zfin


STEP_UNROLL = 8


def _gdn_step_kernel(s_ref, k_ref, q_ref, v_ref, g_ref, beta_ref, s_out_ref, o_ref):
    dk = s_ref.shape[1]
    g = g_ref[0]

    def project(i, u):
        return u + s_ref[0, i] * k_ref[0, pl.ds(i, 1), :]

    u = lax.fori_loop(0, dk, project, jnp.zeros(v_ref.shape[1:], F32), unroll=STEP_UNROLL)
    resid = beta_ref[0] * (v_ref[0] - g * u)

    def update(i, o):
        s_new = g * s_ref[0, i] + k_ref[0, pl.ds(i, 1), :] * resid
        s_out_ref[0, i] = s_new
        return o + q_ref[0, pl.ds(i, 1), :] * s_new

    o_ref[0] = lax.fori_loop(0, dk, update, jnp.zeros(v_ref.shape[1:], F32), unroll=STEP_UNROLL)


def gdn_step(s, k, q, v, g, beta):
    h, dk, dv, b = s.shape
    sblk = pl.BlockSpec((1, dk, dv, b), lambda i: (i, 0, 0, 0))
    kblk = pl.BlockSpec((1, dk, b), lambda i: (i, 0, 0))
    vblk = pl.BlockSpec((1, dv, b), lambda i: (i, 0, 0))
    one = pl.BlockSpec((1, 1, b), lambda i: (i, 0, 0))
    return pl.pallas_call(
        _gdn_step_kernel,
        grid=(h,),
        in_specs=[sblk, kblk, kblk, vblk, one, one],
        out_specs=[sblk, vblk],
        out_shape=[jax.ShapeDtypeStruct(s.shape, F32), jax.ShapeDtypeStruct(v.shape, F32)],
        compiler_params=_params("parallel"),
        name="gdn_step",
    )(s, k, q, v, g, beta)


def _rwkv_step_kernel(s_ref, c_ref, r_ref, w_ref, bb_ref, kw_ref, vv_ref, s_out_ref, y_ref):
    nv = s_ref.shape[1]
    c = c_ref[0]
    r = r_ref[0]
    w = w_ref[0]
    bb = bb_ref[0]
    kw = kw_ref[0]
    wr = w * r
    bb_r = jnp.sum(bb * r, axis=0, keepdims=True)
    kw_r = jnp.sum(kw * r, axis=0, keepdims=True)

    def body(i, carry):
        s = s_ref[0, i]
        sa = -jnp.sum(s * c, axis=0, keepdims=True)
        y0 = jnp.sum(s * wr, axis=0, keepdims=True)
        vrow = vv_ref[0, pl.ds(i, 1), :]
        s_out_ref[0, i] = s * w + sa * bb + vrow * kw
        y_ref[0, pl.ds(i, 1), :] = y0 + sa * bb_r + vrow * kw_r
        return carry

    lax.fori_loop(0, nv, body, 0, unroll=STEP_UNROLL)


def rwkv_step(s, c, r, w, bb, kw, vv):
    h, nv, nk, b = s.shape
    sblk = pl.BlockSpec((1, nv, nk, b), lambda i: (i, 0, 0, 0))
    kblk = pl.BlockSpec((1, nk, b), lambda i: (i, 0, 0))
    vblk = pl.BlockSpec((1, nv, b), lambda i: (i, 0, 0))
    return pl.pallas_call(
        _rwkv_step_kernel,
        grid=(h,),
        in_specs=[sblk, kblk, kblk, kblk, kblk, kblk, vblk],
        out_specs=[sblk, vblk],
        out_shape=[jax.ShapeDtypeStruct(s.shape, F32), jax.ShapeDtypeStruct(vv.shape, F32)],
        compiler_params=_params("parallel"),
        name="rwkv_step",
    )(s, c, r, w, bb, kw, vv)


ROUTER_TILE = 512


def _router_kernel(x_ref, w_ref, b_ref, o_ref):
    logits = _dot3(x_ref[...], w_ref[...]) + b_ref[...]
    lane = lax.broadcasted_iota(jnp.int32, logits.shape, 1)
    ninf = -jnp.inf
    big = jnp.int32(LANES)
    g_logit = jnp.where(lane < N_GROUPS, logits, ninf)
    g_max = jnp.max(g_logit, axis=-1, keepdims=True)
    grp = jnp.min(jnp.where(g_logit == g_max, lane, big), axis=-1, keepdims=True)
    g_prob = 1.0 / jnp.sum(jnp.exp(g_logit - g_max), axis=-1, keepdims=True)
    lo = N_GROUPS + grp * EXP_PER_GROUP
    e_sel = jnp.where((lane >= lo) & (lane < lo + EXP_PER_GROUP), logits, ninf)
    v1 = jnp.max(e_sel, axis=-1, keepdims=True)
    i1 = jnp.min(jnp.where(e_sel == v1, lane, big), axis=-1, keepdims=True)
    e_rest = jnp.where(lane == i1, ninf, e_sel)
    v2 = jnp.max(e_rest, axis=-1, keepdims=True)
    i2 = jnp.min(jnp.where(e_rest == v2, lane, big), axis=-1, keepdims=True)
    e21 = jnp.exp(v2 - v1)
    w1 = g_prob / (1.0 + e21)
    w2 = g_prob * e21 / (1.0 + e21)
    out = jnp.where(lane == 0, (i1 - N_GROUPS).astype(F32),
                    jnp.where(lane == 1, (i2 - N_GROUPS).astype(F32),
                              jnp.where(lane == 2, w1, jnp.where(lane == 3, w2, 0.0))))
    o_ref[...] = out


def moe_route(x, w_grp, b_grp, w_exp, b_exp):
    n, d = x.shape
    tm = min(ROUTER_TILE, n)
    pad = LANES - N_GROUPS - N_EXPERTS
    w = jnp.concatenate([w_grp, w_exp, jnp.zeros((d, pad), F32)], axis=1)
    bias = jnp.concatenate([b_grp, b_exp, jnp.zeros((pad,), F32)])[None, :]
    out = pl.pallas_call(
        _router_kernel,
        grid=(n // tm,),
        in_specs=[pl.BlockSpec((tm, d), lambda i: (i, 0)),
                  pl.BlockSpec((d, LANES), lambda i: (0, 0)),
                  pl.BlockSpec((1, LANES), lambda i: (0, 0))],
        out_specs=pl.BlockSpec((tm, LANES), lambda i: (i, 0)),
        out_shape=jax.ShapeDtypeStruct((n, LANES), F32),
        compiler_params=_params("parallel"),
        name="moe_router",
    )(x, w, bias)
    return out[:, 0:2].astype(jnp.int32), out[:, 2:4]


def _expert_kernel(te_ref, x_ref, wg_ref, wu_ref, wd_ref, o_ref):
    del te_ref
    x = x_ref[...].astype(BF16)
    gate = _dot(x, wg_ref[0])
    up = _dot(x, wu_ref[0])
    hid = jax.nn.silu(gate) * up
    o_ref[...] = _dot(hid, wd_ref[0])


def moe_experts(x_sorted, tile_expert, w_gate, w_up, w_down, tm):
    p, d = x_sorted.shape
    f = w_gate.shape[2]
    grid_spec = pltpu.PrefetchScalarGridSpec(
        num_scalar_prefetch=1,
        grid=(p // tm,),
        in_specs=[pl.BlockSpec((tm, d), lambda i, te: (i, 0)),
                  pl.BlockSpec((1, d, f), lambda i, te: (te[i], 0, 0)),
                  pl.BlockSpec((1, d, f), lambda i, te: (te[i], 0, 0)),
                  pl.BlockSpec((1, f, d), lambda i, te: (te[i], 0, 0))],
        out_specs=pl.BlockSpec((tm, d), lambda i, te: (i, 0)),
    )
    return pl.pallas_call(
        _expert_kernel,
        grid_spec=grid_spec,
        out_shape=jax.ShapeDtypeStruct((p, d), F32),
        compiler_params=_params("arbitrary"),
        name="moe_experts",
    )(tile_expert, x_sorted, w_gate, w_up, w_down)


def hier_moe(x, w_grp, b_grp, w_exp, b_exp, w_gate, w_up, w_down):
    n, d = x.shape
    eid, wts = moe_route(x, w_grp, b_grp, w_exp, b_exp)
    tm = 256 if n >= 4096 else 32
    n_assign = 2 * n
    p = n_assign + N_EXPERTS * tm
    e_flat = eid.reshape(-1)
    onehot = (e_flat[:, None] == jnp.arange(N_EXPERTS, dtype=jnp.int32)[None, :]).astype(jnp.int32)
    counts = jnp.sum(onehot, axis=0)
    rank = jnp.sum((jnp.cumsum(onehot, axis=0) - onehot) * onehot, axis=1)
    padded = (counts + tm - 1) // tm * tm
    seg_end = jnp.cumsum(padded)
    seg_start = seg_end - padded
    dest = seg_start[e_flat] + rank
    src_token = jnp.full((p,), n, jnp.int32).at[dest].set(jnp.arange(n_assign, dtype=jnp.int32) // 2)
    x_pad = jnp.concatenate([x, jnp.zeros((1, d), x.dtype)], axis=0)
    x_sorted = x_pad[src_token]
    tile_start = jnp.arange(p // tm, dtype=jnp.int32) * tm
    tile_expert = jnp.sum((tile_start[:, None] >= seg_end[None, :]).astype(jnp.int32), axis=1)
    tile_expert = jnp.minimum(tile_expert, N_EXPERTS - 1)
    out_sorted = moe_experts(x_sorted, tile_expert, w_gate, w_up, w_down, tm)
    picked = out_sorted[dest].reshape(n, 2, d)
    return jnp.sum(picked * wts[:, :, None], axis=1)


def _rmsnorm(x, g):
    return x * lax.rsqrt(jnp.mean(x * x, axis=-1, keepdims=True) + RMS_EPS) * g


def _l2norm(x):
    return x * lax.rsqrt(jnp.sum(x * x, axis=-1, keepdims=True) + RMS_EPS)


def _heads(t):
    return t.reshape(t.shape[:-1] + (-1, HEAD_DIM))


def _seq_major(t):
    b, s, h, dh = t.shape
    return t.transpose(0, 2, 1, 3).reshape(b * h, s, dh)


def _lane_major(t):
    return t.transpose(1, 2, 0)


def _ab_mixer(h, p, past):
    b, t, d = h.shape
    z = mm(h.reshape(b * t, d), p["w_in"]).reshape(b, t, -1)
    qa, ka, va, fa, qb, kb, vb, gb, beta_in, alpha_in = jnp.split(z, IN_OFFSETS, axis=-1)
    lf = jax.nn.log_sigmoid(fa + p["b_f"])
    qkv_b = jnp.concatenate([qb, kb, vb], axis=-1)
    if past is None:
        d_cum = cumsum_rows(lf.transpose(0, 2, 1).reshape(b * H_A, t)).reshape(b, H_A, t)
        to_heads = lambda a: _heads(a).transpose(0, 2, 1, 3)
        o_a = fox_prompt(to_heads(qa), to_heads(ka), to_heads(va), d_cum)
        o_a = o_a.transpose(0, 2, 1, 3).reshape(b, t, W_A)
        conv_in = jnp.pad(qkv_b, ((0, 0), (CONV_W - 1, 0), (0, 0)))
    else:
        page_table, pool_k, pool_v, pool_lf, ssm, conv_buf = past
        o_a = fox_decode(_heads(qa)[:, 0], _heads(ka)[:, 0], _heads(va)[:, 0], lf[:, 0], page_table,
                         pool_k.transpose(0, 2, 3, 1), pool_v.transpose(0, 2, 3, 1), pool_lf.transpose(0, 2, 1))
        o_a = o_a.reshape(b, 1, W_A)
        conv_in = jnp.concatenate([conv_buf, qkv_b], axis=1)
    conv_out = jax.nn.silu(sum(conv_in[:, i:i + t] * p["conv_w"][i] for i in range(CONV_W)))
    new_conv = conv_in[:, -(CONV_W - 1):]
    qc, kc, vc = jnp.split(conv_out, 3, axis=-1)
    qn = _l2norm(_heads(qc)) * (HEAD_DIM ** -0.5)
    kn = _l2norm(_heads(kc))
    vn = _heads(vc)
    beta = jax.nn.sigmoid(beta_in)
    log_g = -jnp.exp(p["a_log"]) * jax.nn.softplus(alpha_in + p["dt_bias"])
    if past is None:
        logw = jnp.broadcast_to(log_g[..., None], (b, t, H_B, HEAD_DIM))
        s0 = jnp.zeros((b * H_B, HEAD_DIM, HEAD_DIM), F32)
        o_b, s_new = dplr_scan(_seq_major(qn), _seq_major(kn * beta[..., None]), _seq_major(vn), None,
                               _seq_major(kn), _seq_major(logw), s0, False)
        o_b = o_b.reshape(b, H_B, t, HEAD_DIM).transpose(0, 2, 1, 3)
        s_new = s_new.reshape(b, H_B, HEAD_DIM, HEAD_DIM)
    else:
        row = lambda a: a[:, 0].T[:, None, :]
        s_new, o_b = gdn_step(ssm.transpose(1, 2, 3, 0), _lane_major(kn[:, 0]), _lane_major(qn[:, 0]),
                              _lane_major(vn[:, 0]), row(jnp.exp(log_g)), row(beta))
        s_new = s_new.transpose(3, 0, 1, 2)
        o_b = o_b.transpose(2, 0, 1)[:, None]
    o_b = _rmsnorm(o_b, p["o_norm"]) * jax.nn.silu(_heads(gb))
    o = jnp.concatenate([o_a, o_b.reshape(b, t, W_B)], axis=-1)
    out = mm(o.reshape(b * t, W_A + W_B), p["w_out"]).reshape(b, t, d)
    return out, (_heads(ka), _heads(va), lf, s_new, new_conv)


def _rwkv7_mixer(h, prev, s0, p, single_step):
    b, t, d = h.shape
    h_prev = jnp.concatenate([prev[:, None, :], h[:, :-1]], axis=1)
    xx = h_prev - h
    xr, xw, xk, xv, xa, xg = (h + xx * p["mu"][i] for i in range(6))
    flat = lambda a: a.reshape(b * t, -1)
    r = mm(flat(xr), p["w_rkv"][0])
    k = mm(flat(xk), p["w_rkv"][1])
    v = mm(flat(xv), p["w_rkv"][2])
    w_log = -jax.nn.softplus(-(p["w0"] + mm(jnp.tanh(mm(flat(xw), p["w1"])), p["w2"]))) - 0.5
    log_decay = -jnp.exp(w_log)
    a = jax.nn.sigmoid(p["a0"] + mm(mm(flat(xa), p["a1"]), p["a2"]))
    g = mm(jax.nn.sigmoid(mm(flat(xg), p["g1"])), p["g2"])
    kk = _l2norm(_heads(k * p["k_k"]))
    k = k * (1 + (a - 1) * p["k_a"])
    r_h, k_h, v_h, a_h = _heads(r), _heads(k), _heads(v), _heads(a)
    if single_step:
        s_fin, y = rwkv_step(s0.transpose(1, 2, 3, 0), _lane_major(kk), _lane_major(r_h),
                             _lane_major(_heads(jnp.exp(log_decay))), _lane_major(kk * a_h),
                             _lane_major(k_h), _lane_major(v_h))
        s_fin = s_fin.transpose(3, 0, 1, 2)
        y = y.transpose(2, 0, 1)
    else:
        sm = lambda z: _seq_major(z.reshape(b, t, H_C, HEAD_DIM))
        z0 = s0.reshape(b * H_C, HEAD_DIM, HEAD_DIM).transpose(0, 2, 1)
        y, zfin = dplr_scan(sm(r_h), sm(k_h), sm(v_h), sm(kk * a_h), sm(kk), sm(log_decay), z0, True)
        s_fin = zfin.transpose(0, 2, 1).reshape(b, H_C, HEAD_DIM, HEAD_DIM)
        y = y.reshape(b, H_C, t, HEAD_DIM).transpose(0, 2, 1, 3).reshape(b * t, H_C, HEAD_DIM)
    mean = jnp.mean(y, axis=-1, keepdims=True)
    var = jnp.mean(jnp.square(y - mean), axis=-1, keepdims=True)
    y = ((y - mean) * lax.rsqrt(var + GN_EPS)).reshape(b * t, d) * p["gn_w"] + p["gn_b"]
    bonus = jnp.sum(r_h * k_h * p["r_k"], axis=-1, keepdims=True) * v_h
    y = y + bonus.reshape(b * t, d)
    out = mm(y * g, p["w_out"]).reshape(b, t, d)
    return out, s_fin, h[:, -1]


def _trunk(x, c, past, P):
    depth = P["w_mod"].shape[0]
    ab_states, rw_states = [], []
    bsz, t, d = x.shape
    for layer in range(depth):
        mod = mm(jax.nn.silu(c), P["w_mod"][layer]) + P["b_mod"][layer]
        sh1, sc1, g1, sh2, sc2, g2 = jnp.split(mod[:, None, :], 6, axis=-1)
        h = _rmsnorm(x, P["norm_mix"][layer]) * (1 + sc1) + sh1
        if layer % 2 == 0:
            i = layer // 2
            lp = {name: arr[i] for name, arr in P["ab"].items()}
            lpast = None if past is None else (past["page_table"], past["cache_k"][i], past["cache_v"][i],
                                               past["cache_lf"][i], past["ssm"][i], past["conv"][i])
            out, st = _ab_mixer(h, lp, lpast)
            ab_states.append(st)
        else:
            j = layer // 2
            lp = {name: arr[j] for name, arr in P["rw"].items()}
            if past is None:
                prev = jnp.zeros((bsz, d), x.dtype)
                s0 = jnp.zeros((bsz, H_C, HEAD_DIM, HEAD_DIM), F32)
            else:
                prev, s0 = past["shift"][j], past["wkv"][j]
            out, s_fin, last = _rwkv7_mixer(h, prev, s0, lp, past is not None)
            rw_states.append((s_fin, last))
        x = x + g1 * out
        h = _rmsnorm(x, P["norm_ffn"][layer]) * (1 + sc2) + sh2
        mp = {name: arr[layer] for name, arr in P["moe"].items()}
        y = hier_moe(h.reshape(bsz * t, d), mp["w_grp"], mp["b_grp"], mp["w_exp"], mp["b_exp"],
                     mp["w_gate"], mp["w_up"], mp["w_down"])
        x = x + g2 * y.reshape(bsz, t, d)
    return _rmsnorm(x, P["norm_final"]), ab_states, rw_states


def _stack(states):
    return tuple(jnp.stack([s[m] for s in states]) for m in range(len(states[0])))


def kernel(x_prompt, x_sample, c_prompt, c_sample, page_table, cache_k_a, cache_v_a, cache_lf_a, state_ssm_b, state_conv_b, state_wkv_c, state_shift_c, w_mod, b_mod, norm_mix, norm_ffn, norm_final, ab_w_in, ab_b_f, ab_conv_w, ab_a_log, ab_dt_bias, ab_o_norm, ab_w_out, rw_mu, rw_w_rkv, rw_w0, rw_w1, rw_w2, rw_a0, rw_a1, rw_a2, rw_g1, rw_g2, rw_k_k, rw_k_a, rw_r_k, rw_gn_w, rw_gn_b, rw_w_out, moe_w_grp, moe_b_grp, moe_w_exp, moe_b_exp, moe_w_gate, moe_w_up, moe_w_down):
    P = {
        "w_mod": w_mod, "b_mod": b_mod, "norm_mix": norm_mix, "norm_ffn": norm_ffn, "norm_final": norm_final,
        "ab": {"w_in": ab_w_in, "b_f": ab_b_f, "conv_w": ab_conv_w, "a_log": ab_a_log,
               "dt_bias": ab_dt_bias, "o_norm": ab_o_norm, "w_out": ab_w_out},
        "rw": {"mu": rw_mu, "w_rkv": rw_w_rkv, "w0": rw_w0, "w1": rw_w1, "w2": rw_w2,
               "a0": rw_a0, "a1": rw_a1, "a2": rw_a2, "g1": rw_g1, "g2": rw_g2,
               "k_k": rw_k_k, "k_a": rw_k_a, "r_k": rw_r_k, "gn_w": rw_gn_w, "gn_b": rw_gn_b,
               "w_out": rw_w_out},
        "moe": {"w_grp": moe_w_grp, "b_grp": moe_b_grp, "w_exp": moe_w_exp, "b_exp": moe_b_exp,
                "w_gate": moe_w_gate, "w_up": moe_w_up, "w_down": moe_w_down},
    }
    y_prompt, ab_p, rw_p = _trunk(x_prompt, c_prompt, None, P)
    past = {"page_table": page_table, "cache_k": cache_k_a, "cache_v": cache_v_a, "cache_lf": cache_lf_a,
            "ssm": state_ssm_b, "conv": state_conv_b, "wkv": state_wkv_c, "shift": state_shift_c}
    y_sample, ab_s, rw_s = _trunk(x_sample, c_sample, past, P)
    return (y_prompt, y_sample) + _stack(ab_p) + _stack(rw_p) + _stack(ab_s) + _stack(rw_s)
```

```python
import functools
import math

import jax
import jax.numpy as jnp
import numpy as np
from jax import lax
from jax.experimental import pallas as pl
from jax.experimental.pallas import tpu as pltpu

F32 = jnp.float32
BF16 = jnp.bfloat16

D_MODEL = 1024
HEAD_DIM = 64
H_A = 8
H_B = 8
W_A = H_A * HEAD_DIM
W_B = H_B * HEAD_DIM
CONV_W = 4
H_C = D_MODEL // HEAD_DIM
N_GROUPS = 4
EXP_PER_GROUP = 8
N_EXPERTS = N_GROUPS * EXP_PER_GROUP
D_EXPERT = D_MODEL // 4
PAGE_SIZE = 128
IN_SIZES = (W_A, W_A, W_A, H_A, W_B, W_B, W_B, W_B, H_B, H_B)
IN_OFFSETS = tuple(int(v) for v in np.cumsum(IN_SIZES)[:-1])
RMS_EPS = 1e-6
GN_EPS = 64e-5

LANES = 128
SUBLANES = 8
CHUNK = 64
VMEM_LIMIT = 48 * 1024 * 1024

_NN = (((1,), (0,)), ((), ()))
_NT = (((1,), (1,)), ((), ()))
_TN = (((0,), (0,)), ((), ()))


def _round_up(n, m):
    return (n + m - 1) // m * m


def _params(*sem):
    return pltpu.CompilerParams(dimension_semantics=sem, vmem_limit_bytes=VMEM_LIMIT)


def _dot(a, b, dims=_NN):
    return lax.dot_general(a.astype(BF16), b.astype(BF16), dims, preferred_element_type=F32)


def _split2(a):
    hi = a.astype(BF16)
    lo = (a - hi.astype(F32)).astype(BF16)
    return hi, lo


def _dot3(a, b, dims=_NN):
    ah, al = _split2(a)
    bh, bl = _split2(b)
    d = functools.partial(lax.dot_general, dimension_numbers=dims, preferred_element_type=F32)
    return d(ah, bh) + (d(ah, bl) + d(al, bh))


def _dot3_rows(a, b, dims=_NN):
    m = a.shape[0]
    ah, al = _split2(a)
    bh, bl = _split2(b)
    d = functools.partial(lax.dot_general, dimension_numbers=dims, preferred_element_type=F32)
    top = d(jnp.concatenate([ah, al], axis=0), bh)
    return top[:m] + top[m:] + d(ah, bl)


def _dot_exact_rhs(a, ones_b, dims=_NN):
    hi = a.astype(BF16)
    r = a - hi.astype(F32)
    mid = r.astype(BF16)
    lo = (r - mid.astype(F32)).astype(BF16)
    d = functools.partial(lax.dot_general, dimension_numbers=dims, preferred_element_type=F32)
    return d(hi, ones_b) + (d(mid, ones_b) + d(lo, ones_b))


def _dot_exact_lhs(ones_a, b, dims=_NN):
    hi = b.astype(BF16)
    r = b - hi.astype(F32)
    mid = r.astype(BF16)
    lo = (r - mid.astype(F32)).astype(BF16)
    d = functools.partial(lax.dot_general, dimension_numbers=dims, preferred_element_type=F32)
    return d(ones_a, hi) + (d(ones_a, mid) + d(ones_a, lo))


def _mm_kernel(x_ref, w_ref, o_ref):
    o_ref[...] = _dot(x_ref[...], w_ref[...])


def mm(x, w):
    m, k = x.shape
    n = w.shape[1]
    tm = min(1024, _round_up(m, SUBLANES))
    mp = _round_up(m, tm)
    np_ = _round_up(n, 2 * LANES)
    tn = next(t for t in (1024, 768, 512, 256) if np_ % t == 0)
    if mp != m:
        x = jnp.pad(x, ((0, mp - m), (0, 0)))
    if np_ != n:
        w = jnp.pad(w, ((0, 0), (0, np_ - n)))
    out = pl.pallas_call(
        _mm_kernel,
        grid=(mp // tm, np_ // tn),
        in_specs=[pl.BlockSpec((tm, k), lambda i, j: (i, 0)),
                  pl.BlockSpec((k, tn), lambda i, j: (0, j))],
        out_specs=pl.BlockSpec((tm, tn), lambda i, j: (i, j)),
        out_shape=jax.ShapeDtypeStruct((mp, np_), F32),
        compiler_params=_params("parallel", "arbitrary"),
        name="mm",
    )(x, w)
    return out[:m, :n]


CUMSUM_TILE = 512


def _cumsum_kernel(x_ref, o_ref, carry_ref):
    @pl.when(pl.program_id(0) == 0)
    def _():
        carry_ref[...] = jnp.zeros_like(carry_ref)

    t = CUMSUM_TILE
    row = lax.broadcasted_iota(jnp.int32, (t, t), 0)
    col = lax.broadcasted_iota(jnp.int32, (t, t), 1)
    upper = jnp.where(row <= col, 1.0, 0.0).astype(BF16)
    c = _dot_exact_rhs(x_ref[...], upper) + carry_ref[:, 0:1]
    o_ref[...] = c
    carry_ref[...] = jnp.broadcast_to(c[:, t - 1:t], carry_ref.shape)


def cumsum_rows(x):
    r, s = x.shape
    return pl.pallas_call(
        _cumsum_kernel,
        grid=(s // CUMSUM_TILE,),
        in_specs=[pl.BlockSpec((r, CUMSUM_TILE), lambda i: (0, i))],
        out_specs=pl.BlockSpec((r, CUMSUM_TILE), lambda i: (0, i)),
        out_shape=jax.ShapeDtypeStruct((r, s), F32),
        scratch_shapes=[pltpu.VMEM((r, LANES), F32)],
        compiler_params=_params("arbitrary"),
        name="cumsum_rows",
    )(x)


ATT_TILE = 512


def _fox_prompt_kernel(q_ref, k_ref, v_ref, dq_ref, dk_ref, o_ref, m_ref, l_ref, acc_ref):
    t = ATT_TILE
    qi = pl.program_id(2)
    q = (q_ref[0, 0] * (HEAD_DIM ** -0.5)).astype(BF16)
    dq = dq_ref[0, 0]
    m_ref[...] = jnp.full_like(m_ref, -jnp.inf)
    l_ref[...] = jnp.zeros_like(l_ref)
    acc_ref[...] = jnp.zeros_like(acc_ref)

    def block(kj, masked):
        start = pl.multiple_of(kj * t, t)
        k = k_ref[0, 0, pl.ds(start, t), :]
        v = v_ref[0, 0, pl.ds(start, t), :]
        dk = dk_ref[0, 0, :, pl.ds(start, t)]
        for half in range(2):
            rs = slice(half * (t // 2), (half + 1) * (t // 2))
            s = _dot(q[rs], k, _NT) + (dq[rs] - dk)
            if masked:
                row = lax.broadcasted_iota(jnp.int32, (t // 2, t), 0) + half * (t // 2)
                col = lax.broadcasted_iota(jnp.int32, (t // 2, t), 1)
                s = jnp.where(col <= row, s, -jnp.inf)
            m_old = m_ref[rs, :]
            m_new = jnp.maximum(m_old, jnp.max(s, axis=-1, keepdims=True))
            alpha = jnp.exp(m_old - m_new)
            p = jnp.exp(s - m_new)
            l_ref[rs, :] = alpha * l_ref[rs, :] + jnp.sum(p, axis=-1, keepdims=True)
            acc_ref[rs, :] = alpha * acc_ref[rs, :] + _dot(p, v)
            m_ref[rs, :] = m_new

    def body(kj, carry):
        block(kj, False)
        return carry

    lax.fori_loop(0, qi, body, 0)
    block(qi, True)
    o_ref[0, 0] = acc_ref[...] / l_ref[...]


def fox_prompt(q, k, v, d):
    b, h, s, dh = q.shape
    t = ATT_TILE
    dq = d[..., None]
    dk = d[:, :, None, :]
    full = pl.BlockSpec((1, 1, s, dh), lambda bi, hi, qi: (bi, hi, 0, 0))
    return pl.pallas_call(
        _fox_prompt_kernel,
        grid=(b, h, s // t),
        in_specs=[pl.BlockSpec((1, 1, t, dh), lambda bi, hi, qi: (bi, hi, qi, 0)),
                  full, full,
                  pl.BlockSpec((1, 1, t, 1), lambda bi, hi, qi: (bi, hi, qi, 0)),
                  pl.BlockSpec((1, 1, 1, s), lambda bi, hi, qi: (bi, hi, 0, 0))],
        out_specs=pl.BlockSpec((1, 1, t, dh), lambda bi, hi, qi: (bi, hi, qi, 0)),
        out_shape=jax.ShapeDtypeStruct((b, h, s, dh), F32),
        scratch_shapes=[pltpu.VMEM((t, 1), F32), pltpu.VMEM((t, 1), F32), pltpu.VMEM((t, dh), F32)],
        compiler_params=_params("parallel", "parallel", "arbitrary"),
        name="fox_prompt",
    )(q, k, v, dq, dk)


DEC_PAGES = 8


def _fox_decode_kernel(pt_ref, qb_ref, knew_ref, vnew_ref, lfnew_ref, *rest):
    del pt_ref
    npg = DEC_PAGES
    k_refs, v_refs, lf_refs = rest[:npg], rest[npg:2 * npg], rest[2 * npg:3 * npg]
    o_ref, m_ref, l_ref, acc_ref, suf_ref = rest[3 * npg:]
    p = pl.program_id(1)
    qb = qb_ref[0]

    @pl.when(p == 0)
    def _():
        m_ref[...] = jnp.sum(knew_ref[0] * qb, axis=1)
        l_ref[...] = jnp.ones_like(l_ref)
        lane = lax.broadcasted_iota(jnp.int32, acc_ref.shape, 2)
        acc_ref[...] = jnp.where(lane == 0, vnew_ref[0], 0.0)
        suf_ref[...] = lfnew_ref[0]

    r = lax.broadcasted_iota(jnp.int32, (PAGE_SIZE, PAGE_SIZE), 0)
    c = lax.broadcasted_iota(jnp.int32, (PAGE_SIZE, PAGE_SIZE), 1)
    after = jnp.where(r > c, 1.0, 0.0).astype(BF16)
    for j in range(npg):
        kt = k_refs[j][0]
        vt = v_refs[j][0]
        lf = lf_refs[j][0]
        suf = suf_ref[...]
        s = jnp.sum(kt * qb, axis=1) + _dot_exact_rhs(lf, after) + suf
        m_old = m_ref[...]
        m_new = jnp.maximum(m_old, jnp.max(s, axis=-1, keepdims=True))
        alpha = jnp.exp(m_old - m_new)
        pr = jnp.exp(s - m_new)
        l_ref[...] = alpha * l_ref[...] + jnp.sum(pr, axis=-1, keepdims=True)
        acc_ref[...] = alpha[:, None, :] * acc_ref[...] + pr[:, None, :] * vt
        m_ref[...] = m_new
        suf_ref[...] = suf + jnp.sum(lf, axis=-1, keepdims=True)

    @pl.when(p == pl.num_programs(1) - 1)
    def _():
        o_ref[0] = jnp.sum(acc_ref[...], axis=-1) / l_ref[:, 0:1]


def fox_decode(q, k_new, v_new, lf_new, page_table, pool_kt, pool_vt, pool_lft):
    b, h, dh = q.shape
    n_pages = page_table.shape[1]
    npg = DEC_PAGES
    pt = page_table.reshape(-1)
    rep = lambda a: jnp.broadcast_to(a[..., None], a.shape + (PAGE_SIZE,))

    def page_map(j, nd):
        def f(bi, pi, pt_ref):
            return (pt_ref[bi * n_pages + (n_pages - 1 - (pi * npg + j))],) + (0,) * nd
        return f

    seq4 = lambda bi, pi, pt_ref: (bi, 0, 0, 0)
    seq3 = lambda bi, pi, pt_ref: (bi, 0, 0)
    vec = pl.BlockSpec((1, h, dh, PAGE_SIZE), seq4)
    grid_spec = pltpu.PrefetchScalarGridSpec(
        num_scalar_prefetch=1,
        grid=(b, n_pages // npg),
        in_specs=([vec, vec, vec, pl.BlockSpec((1, h, PAGE_SIZE), seq3)]
                  + [pl.BlockSpec((1, h, dh, PAGE_SIZE), page_map(j, 3)) for j in range(npg)]
                  + [pl.BlockSpec((1, h, dh, PAGE_SIZE), page_map(j, 3)) for j in range(npg)]
                  + [pl.BlockSpec((1, h, PAGE_SIZE), page_map(j, 2)) for j in range(npg)]),
        out_specs=pl.BlockSpec((1, h, dh), seq3),
        scratch_shapes=[pltpu.VMEM((h, PAGE_SIZE), F32), pltpu.VMEM((h, PAGE_SIZE), F32),
                        pltpu.VMEM((h, dh, PAGE_SIZE), F32), pltpu.VMEM((h, PAGE_SIZE), F32)],
    )
    return pl.pallas_call(
        _fox_decode_kernel,
        grid_spec=grid_spec,
        out_shape=jax.ShapeDtypeStruct((b, h, dh), F32),
        compiler_params=_params("parallel", "arbitrary"),
        name="fox_decode",
    )(pt, rep(q * (HEAD_DIM ** -0.5)), rep(k_new), rep(v_new), rep(lf_new),
      *([pool_kt] * npg), *([pool_vt] * npg), *([pool_lft] * npg))


def _chunk_summaries(qs, kws, vs, bs, cs, lws, lrs, vector_decay):
    n = CHUNK
    dk = qs[0].shape[1]
    idx = range(len(qs))
    row = lax.broadcasted_iota(jnp.int32, (n, n), 0)
    col = lax.broadcasted_iota(jnp.int32, (n, n), 1)
    strict = row > col
    incl = row >= col
    lower = jnp.where(incl, 1.0, 0.0).astype(BF16)
    eye = jnp.where(row == col, 1.0, 0.0)
    lg = [_dot_exact_lhs(lower, lws[g]) for g in idx]
    lgm = [lg[g] - lws[g] for g in idx]
    lg_end = [lg[g][n - 1:n, :] for g in idx]
    c_t = [cs[g] * jnp.exp(lgm[g]) for g in idx]
    q_t = [qs[g] * jnp.exp(lg[g]) for g in idx]
    if vector_decay:
        e_neg = [jnp.exp(-lg[g]) for g in idx]
        e_end = [jnp.exp(lg_end[g] - lg[g]) for g in idx]
        kd = [kws[g] * e_end[g] for g in idx]
        bd = [bs[g] * e_end[g] for g in idx]
        rhs = [jnp.concatenate([bs[g] * e_neg[g], kws[g] * e_neg[g]], axis=0) for g in idx]
        top = [_dot3_rows(c_t[g], rhs[g], _NT) for g in idx]
        bot = [_dot(q_t[g], rhs[g], _NT) for g in idx]
        a_mat = [jnp.where(strict, top[g][:, :n], 0.0) for g in idx]
        b_mat = [jnp.where(strict, top[g][:, n:], 0.0) for g in idx]
        nq = [jnp.where(incl, bot[g][:, :n], 0.0) for g in idx]
        mq = [jnp.where(incl, bot[g][:, n:], 0.0) for g in idx]
    else:
        upper = jnp.where(row <= col, 1.0, 0.0).astype(BF16)
        ninf = -jnp.inf
        kd = [kws[g] * jnp.exp(lg_end[g] - lg[g]) for g in idx]
        bd = [kws[g] * jnp.exp(lg_end[g] - lgm[g]) for g in idx]
        g1 = [_dot3_rows(cs[g], kws[g], _NT) for g in idx]
        g2 = [_dot(qs[g], kws[g], _NT) for g in idx]
        a_mat, b_mat, nq, mq = [], [], [], []
        for g in idx:
            lgr = _dot_exact_rhs(lrs[g], upper)
            lgmr = lgr - lrs[g]
            li = jnp.broadcast_to(lg[g][:, 0:1], (n, n))
            lmi = jnp.broadcast_to(lgm[g][:, 0:1], (n, n))
            lj = jnp.broadcast_to(lgr[0:1, :], (n, n))
            lmj = jnp.broadcast_to(lgmr[0:1, :], (n, n))
            a_mat.append(g1[g] * jnp.exp(jnp.where(strict, lmi - lmj, ninf)))
            b_mat.append(g1[g] * jnp.exp(jnp.where(strict, lmi - lj, ninf)))
            mq.append(g2[g] * jnp.exp(jnp.where(incl, li - lj, ninf)))
            nq.append(g2[g] * jnp.exp(jnp.where(incl, li - lmj, ninf)))
    pw = [-a_mat[g] for g in idx]
    t_inv = [eye + pw[g] for g in idx]
    pw = [_dot3_rows(pw[g], pw[g]) for g in idx]
    for _ in range(int(math.log2(n)) - 2):
        res = [_dot3_rows(pw[g], jnp.concatenate([pw[g], t_inv[g]], axis=1)) for g in idx]
        pw = [res[g][:, :n] for g in idx]
        t_inv = [t_inv[g] + res[g][:, n:] for g in idx]
    t_inv = [t_inv[g] + _dot3_rows(pw[g], t_inv[g]) for g in idx]
    bmv = [_dot3_rows(b_mat[g], vs[g]) for g in idx]
    w12 = [_dot3_rows(t_inv[g], jnp.concatenate([c_t[g], bmv[g]], axis=1)) for g in idx]
    zero = jnp.zeros((n, dk), F32)
    rmat = [jnp.concatenate([w12[g], jnp.concatenate([zero, -vs[g]], axis=1)], axis=0) for g in idx]
    res6 = [_dot(jnp.concatenate([nq[g], mq[g]], axis=1), rmat[g]) for g in idx]
    res7 = [_dot3(jnp.concatenate([bd[g], kd[g]], axis=0), rmat[g], _TN) for g in idx]
    out = []
    for g in idx:
        qeff = q_t[g] - res6[g][:, :dk]
        oloc = -res6[g][:, dk:]
        aeff = eye * jnp.exp(lg_end[g]) - res7[g][:, :dk]
        zloc = -res7[g][:, dk:]
        out.append((qeff, oloc, aeff, zloc))
    return out


def _chunk_kernel(*refs, vector_decay, chunks):
    if vector_decay:
        q_ref, kw_ref, v_ref, b_ref, c_ref, lw_ref, qeff_ref, oloc_ref, aeff_ref, zloc_ref = refs
        lr_ref = None
    else:
        q_ref, kw_ref, v_ref, c_ref, lw_ref, lr_ref, qeff_ref, oloc_ref, aeff_ref, zloc_ref = refs
        b_ref = None
    sls = [slice(g * CHUNK, (g + 1) * CHUNK) for g in range(chunks)]
    take = lambda ref: [ref[0, sl, :] for sl in sls]
    outs = _chunk_summaries(take(q_ref), take(kw_ref), take(v_ref),
                            take(b_ref) if vector_decay else None, take(c_ref), take(lw_ref),
                            None if vector_decay else [lr_ref[0, g] for g in range(chunks)],
                            vector_decay)
    for sl, out in zip(sls, outs):
        qeff_ref[0, sl, :], oloc_ref[0, sl, :], aeff_ref[0, sl, :], zloc_ref[0, sl, :] = out


def _scan_kernel(z0_ref, qeff_ref, oloc_ref, aeff_ref, zloc_ref, o_ref, zfin_ref, z_ref, *, seqs):
    ci = pl.program_id(1)

    @pl.when(ci == 0)
    def _():
        z_ref[...] = z0_ref[...]

    zs = [z_ref[s] for s in range(seqs)]
    for s in range(seqs):
        o_ref[s] = _dot3_rows(qeff_ref[s], zs[s]) + oloc_ref[s]
    for s in range(seqs):
        z_ref[s] = _dot3_rows(aeff_ref[s], zs[s]) + zloc_ref[s]

    @pl.when(ci == pl.num_programs(1) - 1)
    def _():
        zfin_ref[...] = z_ref[...]


SCAN_SEQS = 16
CHUNKS_PER_STEP = 16


def dplr_scan(q, kw, v, bvec, c, logw, z0, vector_decay):
    n, t, dh = q.shape
    nc = t // CHUNK
    g = CHUNKS_PER_STEP
    assert t % (g * CHUNK) == 0 and n % SCAN_SEQS == 0
    blk = pl.BlockSpec((1, g * CHUNK, dh), lambda i, j: (i, j, 0))
    ins = [q, kw, v] + ([bvec] if vector_decay else []) + [c, logw]
    in_specs = [blk] * len(ins)
    if not vector_decay:
        lrow = jnp.broadcast_to(logw[:, :, 0].reshape(n, nc, 1, CHUNK), (n, nc, SUBLANES, CHUNK))
        ins.append(lrow)
        in_specs.append(pl.BlockSpec((1, g, SUBLANES, CHUNK), lambda i, j: (i, j, 0, 0)))
    shp = jax.ShapeDtypeStruct((n, t, dh), F32)
    qeff, oloc, aeff, zloc = pl.pallas_call(
        functools.partial(_chunk_kernel, vector_decay=vector_decay, chunks=g),
        grid=(n, nc // g),
        in_specs=in_specs,
        out_specs=[blk] * 4,
        out_shape=[shp] * 4,
        compiler_params=_params("parallel", "parallel"),
        name="dplr_chunk_vec" if vector_decay else "dplr_chunk_scalar",
    )(*ins)
    ns = SCAN_SEQS
    sblk = pl.BlockSpec((ns, CHUNK, dh), lambda i, j: (i, j, 0))
    zblk = pl.BlockSpec((ns, dh, dh), lambda i, j: (i, 0, 0))
    o, zfin = pl.pallas_call(
        functools.partial(_scan_kernel, seqs=ns),
        grid=(n // ns, nc),
        in_specs=[zblk, sblk, sblk, sblk, sblk],
        out_specs=[sblk, zblk],
        out_shape=[shp, jax.ShapeDtypeStruct((n, dh, dh), F32)],
        scratch_shapes=[pltpu.VMEM((ns, dh, dh), F32)],
        compiler_params=_params("parallel", "arbitrary"),
        name="dplr_scan",
    )(z0, qeff, oloc, aeff, zloc)
    return o, zfin


STEP_UNROLL = 8


def _gdn_step_kernel(s_ref, k_ref, q_ref, v_ref, g_ref, beta_ref, s_out_ref, o_ref):
    dk = s_ref.shape[1]
    g = g_ref[0]

    def project(i, u):
        return u + s_ref[0, i] * k_ref[0, pl.ds(i, 1), :]

    u = lax.fori_loop(0, dk, project, jnp.zeros(v_ref.shape[1:], F32), unroll=STEP_UNROLL)
    resid = beta_ref[0] * (v_ref[0] - g * u)

    def update(i, o):
        s_new = g * s_ref[0, i] + k_ref[0, pl.ds(i, 1), :] * resid
        s_out_ref[0, i] = s_new
        return o + q_ref[0, pl.ds(i, 1), :] * s_new

    o_ref[0] = lax.fori_loop(0, dk, update, jnp.zeros(v_ref.shape[1:], F32), unroll=STEP_UNROLL)


def gdn_step(s, k, q, v, g, beta):
    h, dk, dv, b = s.shape
    sblk = pl.BlockSpec((1, dk, dv, b), lambda i: (i, 0, 0, 0))
    kblk = pl.BlockSpec((1, dk, b), lambda i: (i, 0, 0))
    vblk = pl.BlockSpec((1, dv, b), lambda i: (i, 0, 0))
    one = pl.BlockSpec((1, 1, b), lambda i: (i, 0, 0))
    return pl.pallas_call(
        _gdn_step_kernel,
        grid=(h,),
        in_specs=[sblk, kblk, kblk, vblk, one, one],
        out_specs=[sblk, vblk],
        out_shape=[jax.ShapeDtypeStruct(s.shape, F32), jax.ShapeDtypeStruct(v.shape, F32)],
        compiler_params=_params("parallel"),
        name="gdn_step",
    )(s, k, q, v, g, beta)


def _rwkv_step_kernel(s_ref, c_ref, r_ref, w_ref, bb_ref, kw_ref, vv_ref, s_out_ref, y_ref):
    nv = s_ref.shape[1]
    c = c_ref[0]
    r = r_ref[0]
    w = w_ref[0]
    bb = bb_ref[0]
    kw = kw_ref[0]
    wr = w * r
    bb_r = jnp.sum(bb * r, axis=0, keepdims=True)
    kw_r = jnp.sum(kw * r, axis=0, keepdims=True)

    def body(i, carry):
        s = s_ref[0, i]
        sa = -jnp.sum(s * c, axis=0, keepdims=True)
        y0 = jnp.sum(s * wr, axis=0, keepdims=True)
        vrow = vv_ref[0, pl.ds(i, 1), :]
        s_out_ref[0, i] = s * w + sa * bb + vrow * kw
        y_ref[0, pl.ds(i, 1), :] = y0 + sa * bb_r + vrow * kw_r
        return carry

    lax.fori_loop(0, nv, body, 0, unroll=STEP_UNROLL)


def rwkv_step(s, c, r, w, bb, kw, vv):
    h, nv, nk, b = s.shape
    sblk = pl.BlockSpec((1, nv, nk, b), lambda i: (i, 0, 0, 0))
    kblk = pl.BlockSpec((1, nk, b), lambda i: (i, 0, 0))
    vblk = pl.BlockSpec((1, nv, b), lambda i: (i, 0, 0))
    return pl.pallas_call(
        _rwkv_step_kernel,
        grid=(h,),
        in_specs=[sblk, kblk, kblk, kblk, kblk, kblk, vblk],
        out_specs=[sblk, vblk],
        out_shape=[jax.ShapeDtypeStruct(s.shape, F32), jax.ShapeDtypeStruct(vv.shape, F32)],
        compiler_params=_params("parallel"),
        name="rwkv_step",
    )(s, c, r, w, bb, kw, vv)


ROUTER_TILE = 512


def _router_kernel(x_ref, w_ref, b_ref, o_ref):
    logits = _dot3(x_ref[...], w_ref[...]) + b_ref[...]
    lane = lax.broadcasted_iota(jnp.int32, logits.shape, 1)
    ninf = -jnp.inf
    big = jnp.int32(LANES)
    g_logit = jnp.where(lane < N_GROUPS, logits, ninf)
    g_max = jnp.max(g_logit, axis=-1, keepdims=True)
    grp = jnp.min(jnp.where(g_logit == g_max, lane, big), axis=-1, keepdims=True)
    g_prob = 1.0 / jnp.sum(jnp.exp(g_logit - g_max), axis=-1, keepdims=True)
    lo = N_GROUPS + grp * EXP_PER_GROUP
    e_sel = jnp.where((lane >= lo) & (lane < lo + EXP_PER_GROUP), logits, ninf)
    v1 = jnp.max(e_sel, axis=-1, keepdims=True)
    i1 = jnp.min(jnp.where(e_sel == v1, lane, big), axis=-1, keepdims=True)
    e_rest = jnp.where(lane == i1, ninf, e_sel)
    v2 = jnp.max(e_rest, axis=-1, keepdims=True)
    i2 = jnp.min(jnp.where(e_rest == v2, lane, big), axis=-1, keepdims=True)
    e21 = jnp.exp(v2 - v1)
    w1 = g_prob / (1.0 + e21)
    w2 = g_prob * e21 / (1.0 + e21)
    out = jnp.where(lane == 0, (i1 - N_GROUPS).astype(F32),
                    jnp.where(lane == 1, (i2 - N_GROUPS).astype(F32),
                              jnp.where(lane == 2, w1, jnp.where(lane == 3, w2, 0.0))))
    o_ref[...] = out


def moe_route(x, w_grp, b_grp, w_exp, b_exp):
    n, d = x.shape
    tm = min(ROUTER_TILE, n)
    pad = LANES - N_GROUPS - N_EXPERTS
    w = jnp.concatenate([w_grp, w_exp, jnp.zeros((d, pad), F32)], axis=1)
    bias = jnp.concatenate([b_grp, b_exp, jnp.zeros((pad,), F32)])[None, :]
    out = pl.pallas_call(
        _router_kernel,
        grid=(n // tm,),
        in_specs=[pl.BlockSpec((tm, d), lambda i: (i, 0)),
                  pl.BlockSpec((d, LANES), lambda i: (0, 0)),
                  pl.BlockSpec((1, LANES), lambda i: (0, 0))],
        out_specs=pl.BlockSpec((tm, LANES), lambda i: (i, 0)),
        out_shape=jax.ShapeDtypeStruct((n, LANES), F32),
        compiler_params=_params("parallel"),
        name="moe_router",
    )(x, w, bias)
    return out[:, 0:2].astype(jnp.int32), out[:, 2:4]


def _expert_kernel(te_ref, x_ref, wg_ref, wu_ref, wd_ref, o_ref):
    del te_ref
    x = x_ref[...].astype(BF16)
    gate = _dot(x, wg_ref[0])
    up = _dot(x, wu_ref[0])
    hid = jax.nn.silu(gate) * up
    o_ref[...] = _dot(hid, wd_ref[0])


def moe_experts(x_sorted, tile_expert, w_gate, w_up, w_down, tm):
    p, d = x_sorted.shape
    f = w_gate.shape[2]
    grid_spec = pltpu.PrefetchScalarGridSpec(
        num_scalar_prefetch=1,
        grid=(p // tm,),
        in_specs=[pl.BlockSpec((tm, d), lambda i, te: (i, 0)),
                  pl.BlockSpec((1, d, f), lambda i, te: (te[i], 0, 0)),
                  pl.BlockSpec((1, d, f), lambda i, te: (te[i], 0, 0)),
                  pl.BlockSpec((1, f, d), lambda i, te: (te[i], 0, 0))],
        out_specs=pl.BlockSpec((tm, d), lambda i, te: (i, 0)),
    )
    return pl.pallas_call(
        _expert_kernel,
        grid_spec=grid_spec,
        out_shape=jax.ShapeDtypeStruct((p, d), F32),
        compiler_params=_params("arbitrary"),
        name="moe_experts",
    )(tile_expert, x_sorted, w_gate, w_up, w_down)


def hier_moe(x, w_grp, b_grp, w_exp, b_exp, w_gate, w_up, w_down):
    n, d = x.shape
    eid, wts = moe_route(x, w_grp, b_grp, w_exp, b_exp)
    tm = 256 if n >= 4096 else 32
    n_assign = 2 * n
    p = n_assign + N_EXPERTS * tm
    e_flat = eid.reshape(-1)
    onehot = (e_flat[:, None] == jnp.arange(N_EXPERTS, dtype=jnp.int32)[None, :]).astype(jnp.int32)
    counts = jnp.sum(onehot, axis=0)
    rank = jnp.sum((jnp.cumsum(onehot, axis=0) - onehot) * onehot, axis=1)
    padded = (counts + tm - 1) // tm * tm
    seg_end = jnp.cumsum(padded)
    seg_start = seg_end - padded
    dest = seg_start[e_flat] + rank
    src_token = jnp.full((p,), n, jnp.int32).at[dest].set(jnp.arange(n_assign, dtype=jnp.int32) // 2)
    x_pad = jnp.concatenate([x, jnp.zeros((1, d), x.dtype)], axis=0)
    x_sorted = x_pad[src_token]
    tile_start = jnp.arange(p // tm, dtype=jnp.int32) * tm
    tile_expert = jnp.sum((tile_start[:, None] >= seg_end[None, :]).astype(jnp.int32), axis=1)
    tile_expert = jnp.minimum(tile_expert, N_EXPERTS - 1)
    out_sorted = moe_experts(x_sorted, tile_expert, w_gate, w_up, w_down, tm)
    picked = out_sorted[dest].reshape(n, 2, d)
    return jnp.sum(picked * wts[:, :, None], axis=1)


def _rmsnorm(x, g):
    return x * lax.rsqrt(jnp.mean(x * x, axis=-1, keepdims=True) + RMS_EPS) * g


def _l2norm(x):
    return x * lax.rsqrt(jnp.sum(x * x, axis=-1, keepdims=True) + RMS_EPS)


def _heads(t):
    return t.reshape(t.shape[:-1] + (-1, HEAD_DIM))


def _seq_major(t):
    b, s, h, dh = t.shape
    return t.transpose(0, 2, 1, 3).reshape(b * h, s, dh)


def _lane_major(t):
    return t.transpose(1, 2, 0)


def _ab_mixer(h, p, past):
    b, t, d = h.shape
    z = mm(h.reshape(b * t, d), p["w_in"]).reshape(b, t, -1)
    qa, ka, va, fa, qb, kb, vb, gb, beta_in, alpha_in = jnp.split(z, IN_OFFSETS, axis=-1)
    lf = jax.nn.log_sigmoid(fa + p["b_f"])
    qkv_b = jnp.concatenate([qb, kb, vb], axis=-1)
    if past is None:
        d_cum = cumsum_rows(lf.transpose(0, 2, 1).reshape(b * H_A, t)).reshape(b, H_A, t)
        to_heads = lambda a: _heads(a).transpose(0, 2, 1, 3)
        o_a = fox_prompt(to_heads(qa), to_heads(ka), to_heads(va), d_cum)
        o_a = o_a.transpose(0, 2, 1, 3).reshape(b, t, W_A)
        conv_in = jnp.pad(qkv_b, ((0, 0), (CONV_W - 1, 0), (0, 0)))
    else:
        page_table, pool_k, pool_v, pool_lf, ssm, conv_buf = past
        o_a = fox_decode(_heads(qa)[:, 0], _heads(ka)[:, 0], _heads(va)[:, 0], lf[:, 0], page_table,
                         pool_k.transpose(0, 2, 3, 1), pool_v.transpose(0, 2, 3, 1), pool_lf.transpose(0, 2, 1))
        o_a = o_a.reshape(b, 1, W_A)
        conv_in = jnp.concatenate([conv_buf, qkv_b], axis=1)
    conv_out = jax.nn.silu(sum(conv_in[:, i:i + t] * p["conv_w"][i] for i in range(CONV_W)))
    new_conv = conv_in[:, -(CONV_W - 1):]
    qc, kc, vc = jnp.split(conv_out, 3, axis=-1)
    qn = _l2norm(_heads(qc)) * (HEAD_DIM ** -0.5)
    kn = _l2norm(_heads(kc))
    vn = _heads(vc)
    beta = jax.nn.sigmoid(beta_in)
    log_g = -jnp.exp(p["a_log"]) * jax.nn.softplus(alpha_in + p["dt_bias"])
    if past is None:
        logw = jnp.broadcast_to(log_g[..., None], (b, t, H_B, HEAD_DIM))
        s0 = jnp.zeros((b * H_B, HEAD_DIM, HEAD_DIM), F32)
        o_b, s_new = dplr_scan(_seq_major(qn), _seq_major(kn * beta[..., None]), _seq_major(vn), None,
                               _seq_major(kn), _seq_major(logw), s0, False)
        o_b = o_b.reshape(b, H_B, t, HEAD_DIM).transpose(0, 2, 1, 3)
        s_new = s_new.reshape(b, H_B, HEAD_DIM, HEAD_DIM)
    else:
        row = lambda a: a[:, 0].T[:, None, :]
        s_new, o_b = gdn_step(ssm.transpose(1, 2, 3, 0), _lane_major(kn[:, 0]), _lane_major(qn[:, 0]),
                              _lane_major(vn[:, 0]), row(jnp.exp(log_g)), row(beta))
        s_new = s_new.transpose(3, 0, 1, 2)
        o_b = o_b.transpose(2, 0, 1)[:, None]
    o_b = _rmsnorm(o_b, p["o_norm"]) * jax.nn.silu(_heads(gb))
    o = jnp.concatenate([o_a, o_b.reshape(b, t, W_B)], axis=-1)
    out = mm(o.reshape(b * t, W_A + W_B), p["w_out"]).reshape(b, t, d)
    return out, (_heads(ka), _heads(va), lf, s_new, new_conv)


def _rwkv7_mixer(h, prev, s0, p, single_step):
    b, t, d = h.shape
    h_prev = jnp.concatenate([prev[:, None, :], h[:, :-1]], axis=1)
    xx = h_prev - h
    xr, xw, xk, xv, xa, xg = (h + xx * p["mu"][i] for i in range(6))
    flat = lambda a: a.reshape(b * t, -1)
    r = mm(flat(xr), p["w_rkv"][0])
    k = mm(flat(xk), p["w_rkv"][1])
    v = mm(flat(xv), p["w_rkv"][2])
    w_log = -jax.nn.softplus(-(p["w0"] + mm(jnp.tanh(mm(flat(xw), p["w1"])), p["w2"]))) - 0.5
    log_decay = -jnp.exp(w_log)
    a = jax.nn.sigmoid(p["a0"] + mm(mm(flat(xa), p["a1"]), p["a2"]))
    g = mm(jax.nn.sigmoid(mm(flat(xg), p["g1"])), p["g2"])
    kk = _l2norm(_heads(k * p["k_k"]))
    k = k * (1 + (a - 1) * p["k_a"])
    r_h, k_h, v_h, a_h = _heads(r), _heads(k), _heads(v), _heads(a)
    if single_step:
        s_fin, y = rwkv_step(s0.transpose(1, 2, 3, 0), _lane_major(kk), _lane_major(r_h),
                             _lane_major(_heads(jnp.exp(log_decay))), _lane_major(kk * a_h),
                             _lane_major(k_h), _lane_major(v_h))
        s_fin = s_fin.transpose(3, 0, 1, 2)
        y = y.transpose(2, 0, 1)
    else:
        sm = lambda z: _seq_major(z.reshape(b, t, H_C, HEAD_DIM))
        z0 = s0.reshape(b * H_C, HEAD_DIM, HEAD_DIM).transpose(0, 2, 1)
        y, zfin = dplr_scan(sm(r_h), sm(k_h), sm(v_h), sm(kk * a_h), sm(kk), sm(log_decay), z0, True)
        s_fin = zfin.transpose(0, 2, 1).reshape(b, H_C, HEAD_DIM, HEAD_DIM)
        y = y.reshape(b, H_C, t, HEAD_DIM).transpose(0, 2, 1, 3).reshape(b * t, H_C, HEAD_DIM)
    mean = jnp.mean(y, axis=-1, keepdims=True)
    var = jnp.mean(jnp.square(y - mean), axis=-1, keepdims=True)
    y = ((y - mean) * lax.rsqrt(var + GN_EPS)).reshape(b * t, d) * p["gn_w"] + p["gn_b"]
    bonus = jnp.sum(r_h * k_h * p["r_k"], axis=-1, keepdims=True) * v_h
    y = y + bonus.reshape(b * t, d)
    out = mm(y * g, p["w_out"]).reshape(b, t, d)
    return out, s_fin, h[:, -1]


def _trunk(x, c, past, P):
    depth = P["w_mod"].shape[0]
    ab_states, rw_states = [], []
    bsz, t, d = x.shape
    for layer in range(depth):
        mod = mm(jax.nn.silu(c), P["w_mod"][layer]) + P["b_mod"][layer]
        sh1, sc1, g1, sh2, sc2, g2 = jnp.split(mod[:, None, :], 6, axis=-1)
        h = _rmsnorm(x, P["norm_mix"][layer]) * (1 + sc1) + sh1
        if layer % 2 == 0:
            i = layer // 2
            lp = {name: arr[i] for name, arr in P["ab"].items()}
            lpast = None if past is None else (past["page_table"], past["cache_k"][i], past["cache_v"][i],
                                               past["cache_lf"][i], past["ssm"][i], past["conv"][i])
            out, st = _ab_mixer(h, lp, lpast)
            ab_states.append(st)
        else:
            j = layer // 2
            lp = {name: arr[j] for name, arr in P["rw"].items()}
            if past is None:
                prev = jnp.zeros((bsz, d), x.dtype)
                s0 = jnp.zeros((bsz, H_C, HEAD_DIM, HEAD_DIM), F32)
            else:
                prev, s0 = past["shift"][j], past["wkv"][j]
            out, s_fin, last = _rwkv7_mixer(h, prev, s0, lp, past is not None)
            rw_states.append((s_fin, last))
        x = x + g1 * out
        h = _rmsnorm(x, P["norm_ffn"][layer]) * (1 + sc2) + sh2
        mp = {name: arr[layer] for name, arr in P["moe"].items()}
        y = hier_moe(h.reshape(bsz * t, d), mp["w_grp"], mp["b_grp"], mp["w_exp"], mp["b_exp"],
                     mp["w_gate"], mp["w_up"], mp["w_down"])
        x = x + g2 * y.reshape(bsz, t, d)
    return _rmsnorm(x, P["norm_final"]), ab_states, rw_states


def _stack(states):
    return tuple(jnp.stack([s[m] for s in states]) for m in range(len(states[0])))


def kernel(x_prompt, x_sample, c_prompt, c_sample, page_table, cache_k_a, cache_v_a, cache_lf_a, state_ssm_b, state_conv_b, state_wkv_c, state_shift_c, w_mod, b_mod, norm_mix, norm_ffn, norm_final, ab_w_in, ab_b_f, ab_conv_w, ab_a_log, ab_dt_bias, ab_o_norm, ab_w_out, rw_mu, rw_w_rkv, rw_w0, rw_w1, rw_w2, rw_a0, rw_a1, rw_a2, rw_g1, rw_g2, rw_k_k, rw_k_a, rw_r_k, rw_gn_w, rw_gn_b, rw_w_out, moe_w_grp, moe_b_grp, moe_w_exp, moe_b_exp, moe_w_gate, moe_w_up, moe_w_down):
    P = {
        "w_mod": w_mod, "b_mod": b_mod, "norm_mix": norm_mix, "norm_ffn": norm_ffn, "norm_final": norm_final,
        "ab": {"w_in": ab_w_in, "b_f": ab_b_f, "conv_w": ab_conv_w, "a_log": ab_a_log,
               "dt_bias": ab_dt_bias, "o_norm": ab_o_norm, "w_out": ab_w_out},
        "rw": {"mu": rw_mu, "w_rkv": rw_w_rkv, "w0": rw_w0, "w1": rw_w1, "w2": rw_w2,
               "a0": rw_a0, "a1": rw_a1, "a2": rw_a2, "g1": rw_g1, "g2": rw_g2,
               "k_k": rw_k_k, "k_a": rw_k_a, "r_k": rw_r_k, "gn_w": rw_gn_w, "gn_b": rw_gn_b,
               "w_out": rw_w_out},
        "moe": {"w_grp": moe_w_grp, "b_grp": moe_b_grp, "w_exp": moe_w_exp, "b_exp": moe_b_exp,
                "w_gate": moe_w_gate, "w_up": moe_w_up, "w_down": moe_w_down},
    }
    y_prompt, ab_p, rw_p = _trunk(x_prompt, c_prompt, None, P)
    past = {"page_table": page_table, "cache_k": cache_k_a, "cache_v": cache_v_a, "cache_lf": cache_lf_a,
            "ssm": state_ssm_b, "conv": state_conv_b, "wkv": state_wkv_c, "shift": state_shift_c}
    y_sample, ab_s, rw_s = _trunk(x_sample, c_sample, past, P)
    return (y_prompt, y_sample) + _stack(ab_p) + _stack(rw_p) + _stack(ab_s) + _stack(rw_s)
```
